```python
import math
import functools
import jax
import jax.numpy as jnp
from jax import lax
import numpy as np

D_MODEL = 1024
BATCH = 4
SEQ = 4096
DEPTH = 4
DEC_BATCH = 32
DEC_SEQ = 4
PAST_LEN = 8192
PAGE_SIZE = 128

N_EVEN = (DEPTH + 1) // 2
N_ODD = DEPTH // 2
CONV_W = 3
A_WIDTH = D_MODEL // 2
DIFF_HEADS = 4
DIFF_QK_DIM = D_MODEL // 16
DIFF_V_DIM = 2 * DIFF_QK_DIM
DIFF_QK_WIDTH = DIFF_HEADS * 2 * DIFF_QK_DIM
B_WIDTH = DIFF_HEADS * DIFF_V_DIM
EVEN_SPLITS = (A_WIDTH, 2 * A_WIDTH, 3 * A_WIDTH, 3 * A_WIDTH + DIFF_QK_WIDTH, 3 * A_WIDTH + 2 * DIFF_QK_WIDTH)
EVEN_IN = 3 * A_WIDTH + 2 * DIFF_QK_WIDTH + B_WIDTH
FOX_HEADS = 8
FOX_HEAD_DIM = D_MODEL // FOX_HEADS
FOX_WIDTH = FOX_HEADS * FOX_HEAD_DIM
ODD_SPLITS = (FOX_WIDTH, 2 * FOX_WIDTH, 3 * FOX_WIDTH)
ODD_IN = 3 * FOX_WIDTH + FOX_HEADS
NUM_BUCKETS = 32
MAX_DISTANCE = 128
D_FF = (11 * D_MODEL) // 4
FORGET_BIAS_INIT = 4.0
QBLOCK = 128
EPS = 1e-6

kernel_name = 'hybrid_conv_diffattn_fox_decode_step'


def rmsnorm(x, g):
    x32 = x.astype(jnp.float32)
    y = x32 * lax.rsqrt(jnp.mean(x32 * x32, axis=-1, keepdims=True) + EPS)
    return (y * g.astype(jnp.float32)).astype(x.dtype)


def causal_dwconv(u, prev, w):
    t = u.shape[1]
    full = jnp.concatenate([prev.astype(u.dtype), u], axis=1)
    y = full[:, 0:t] * w[0]
    for j in range(1, CONV_W):
        y = y + full[:, j:j + t] * w[j]
    return y, full[:, t:]


def gather_pages(pool, page_table):
    rows = pool[page_table]
    return rows.reshape((rows.shape[0], rows.shape[1] * rows.shape[2]) + rows.shape[3:])


def t5_bucket(rel):
    n = jnp.maximum(rel, 0)
    max_exact = NUM_BUCKETS // 2
    large = max_exact + (jnp.log(jnp.maximum(n, 1).astype(jnp.float32) / max_exact)
                         / math.log(MAX_DISTANCE / max_exact) * (NUM_BUCKETS - max_exact)).astype(jnp.int32)
    large = jnp.minimum(large, NUM_BUCKETS - 1)
    return jnp.where(n < max_exact, n, large)


def blocked_queries(fn, q_arrays, q_pos):
    n_q = q_pos.shape[0]
    blk = QBLOCK if n_q % QBLOCK == 0 else n_q
    n_blk = n_q // blk
    if n_blk == 1:
        return fn(*q_arrays, q_pos)

    def one_block(i):
        start = i * blk
        qb = [lax.dynamic_slice_in_dim(a, start, blk, axis=1) for a in q_arrays]
        return fn(*qb, lax.dynamic_slice_in_dim(q_pos, start, blk, axis=0))

    out = lax.map(one_block, jnp.arange(n_blk))
    out = jnp.moveaxis(out, 0, 1)
    return out.reshape((out.shape[0], n_q) + out.shape[3:])


def diff_attention(q, q_pos, k, v, k_pos, rel_bias, lam):
    d = DIFF_QK_DIM
    rel = q_pos[:, None] - k_pos[None, :]
    valid = rel >= 0
    bias = jnp.transpose(rel_bias[t5_bucket(rel)], (2, 0, 1)).astype(jnp.float32)

    def probs(qa, ka):
        s = jnp.einsum('bqhd,bkhd->bhqk', qa, ka).astype(jnp.float32) * (d ** -0.5) + bias
        return jax.nn.softmax(jnp.where(valid, s, -jnp.inf), axis=-1)

    w = probs(q[..., :d], k[..., :d]) - lam * probs(q[..., d:], k[..., d:])
    return jnp.einsum('bhqk,bkhd->bqhd', w.astype(v.dtype), v)


def fox_attention(q, c_q, q_pos, k, c_k, v, k_pos):
    s = jnp.einsum('bqhd,bkhd->bhqk', q, k).astype(jnp.float32) * (FOX_HEAD_DIM ** -0.5)
    decay = jnp.transpose(c_q, (0, 2, 1))[:, :, :, None] - jnp.transpose(c_k, (0, 2, 1))[:, :, None, :]
    valid = q_pos[:, None] >= k_pos[None, :]
    p = jax.nn.softmax(jnp.where(valid, s + decay, -jnp.inf), axis=-1)
    return jnp.einsum('bhqk,bkhd->bqhd', p.astype(v.dtype), v)


def _run_group(x, past, weights):
    (norm_mix_g, norm_ffn_g, norm_final_g, w_in_even, w_out_even, conv_a_w,
     lam_q1, lam_k1, lam_q2, lam_k2, subln_g, rel_bias,
     w_in_odd, b_forget, w_out_odd, w_gate_ffn, w_up_ffn, conv_ffn_w, w_down_ffn) = weights
    n_seq, t, _ = x.shape
    if past is None:
        past_len = 0
    else:
        (cache_k_diff, cache_v_diff, cache_k_fox, cache_v_fox, cache_logf_fox,
         state_conv_a, state_conv_ffn, page_table) = past
        past_len = page_table.shape[1] * cache_k_diff.shape[2]
    q_pos = past_len + jnp.arange(t, dtype=jnp.int32)
    k_pos = jnp.arange(past_len + t, dtype=jnp.int32)

    k_diff, v_diff, conv_a, k_fox, v_fox, logf_fox, conv_ffn = [], [], [], [], [], [], []
    for l in range(DEPTH):
        h = rmsnorm(x, norm_mix_g[l])
        if l % 2 == 0:
            e = l // 2
            z = h @ w_in_even[e]
            a_b, a_c, a_h, q, k, v = jnp.split(z, EVEN_SPLITS, axis=-1)
            prev_a = jnp.zeros((n_seq, CONV_W - 1, A_WIDTH), x.dtype) if past is None else state_conv_a[e]
            conv_out, conv_state = causal_dwconv(a_c * a_h, prev_a, conv_a_w[e])
            y_a = a_b * conv_out
            q = q.reshape(n_seq, t, DIFF_HEADS, 2 * DIFF_QK_DIM)
            k = k.reshape(n_seq, t, DIFF_HEADS, 2 * DIFF_QK_DIM)
            v = v.reshape(n_seq, t, DIFF_HEADS, DIFF_V_DIM)
            if past is None:
                k_all, v_all = k, v
            else:
                k_all = jnp.concatenate([gather_pages(cache_k_diff[e], page_table).astype(k.dtype), k], axis=1)
                v_all = jnp.concatenate([gather_pages(cache_v_diff[e], page_table).astype(v.dtype), v], axis=1)
            lam_init = 0.8 - 0.6 * math.exp(-0.3 * l)
            lam = (jnp.exp(jnp.sum(lam_q1[e].astype(jnp.float32) * lam_k1[e].astype(jnp.float32)))
                   - jnp.exp(jnp.sum(lam_q2[e].astype(jnp.float32) * lam_k2[e].astype(jnp.float32)))
                   + lam_init)
            attn = functools.partial(diff_attention, k=k_all, v=v_all, k_pos=k_pos, rel_bias=rel_bias, lam=lam)
            o_b = blocked_queries(lambda qb, pb: attn(qb, pb), (q,), q_pos)
            o_b = rmsnorm(o_b, subln_g[e]) * (1.0 - lam_init)
            y = jnp.concatenate([y_a, o_b.reshape(n_seq, t, B_WIDTH)], axis=-1) @ w_out_even[e]
            k_diff.append(k)
            v_diff.append(v)
            conv_a.append(conv_state)
        else:
            o = l // 2
            z = h @ w_in_odd[o]
            q, k, v, f_logit = jnp.split(z, ODD_SPLITS, axis=-1)
            logf = jax.nn.log_sigmoid(f_logit.astype(jnp.float32) + b_forget[o].astype(jnp.float32))
            q = q.reshape(n_seq, t, FOX_HEADS, FOX_HEAD_DIM)
            k = k.reshape(n_seq, t, FOX_HEADS, FOX_HEAD_DIM)
            v = v.reshape(n_seq, t, FOX_HEADS, FOX_HEAD_DIM)
            if past is None:
                k_all, v_all, logf_all = k, v, logf
            else:
                k_all = jnp.concatenate([gather_pages(cache_k_fox[o], page_table).astype(k.dtype), k], axis=1)
                v_all = jnp.concatenate([gather_pages(cache_v_fox[o], page_table).astype(v.dtype), v], axis=1)
                logf_all = jnp.concatenate(
                    [gather_pages(cache_logf_fox[o], page_table).astype(jnp.float32), logf], axis=1)
            c_all = jnp.cumsum(logf_all, axis=1)
            c_q = c_all[:, past_len:]
            attn = functools.partial(fox_attention, k=k_all, c_k=c_all, v=v_all, k_pos=k_pos)
            o_c = blocked_queries(lambda qb, cb, pb: attn(qb, cb, pb), (q, c_q), q_pos)
            y = o_c.reshape(n_seq, t, FOX_WIDTH) @ w_out_odd[o]
            k_fox.append(k)
            v_fox.append(v)
            logf_fox.append(logf)
        x = x + y
        h = rmsnorm(x, norm_ffn_g[l])
        prev_f = jnp.zeros((n_seq, CONV_W - 1, D_FF), x.dtype) if past is None else state_conv_ffn[l]
        g_conv, ffn_state = causal_dwconv(h @ w_gate_ffn[l], prev_f, conv_ffn_w[l])
        x = x + (jax.nn.silu(g_conv) * (h @ w_up_ffn[l])) @ w_down_ffn[l]
        conv_ffn.append(ffn_state)
    y_out = rmsnorm(x, norm_final_g)
    return y_out, (jnp.stack(k_diff), jnp.stack(v_diff), jnp.stack(k_fox), jnp.stack(v_fox),
                   jnp.stack(logf_fox), jnp.stack(conv_a), jnp.stack(conv_ffn))


def setup_inputs(seed: int = 0) -> dict:
    key = jax.random.key(seed)
    ks = iter(jax.random.split(key, 40))

    def nrm(shape, scale):
        return jax.random.normal(next(ks), shape, jnp.float32) * scale

    n_pages = PAST_LEN // PAGE_SIZE
    n_used = DEC_BATCH * n_pages
    n_phys = n_used + max(1, n_used // 4)
    page_table = jax.random.permutation(next(ks), n_phys)[:n_used].reshape(DEC_BATCH, n_pages).astype(jnp.int32)

    x_prompt = nrm((BATCH, SEQ, D_MODEL), 1.0)
    x_sample = nrm((DEC_BATCH, DEC_SEQ, D_MODEL), 1.0)
    cache_k_diff = nrm((N_EVEN, n_phys, PAGE_SIZE, DIFF_HEADS, 2 * DIFF_QK_DIM), 1.0)
    cache_v_diff = nrm((N_EVEN, n_phys, PAGE_SIZE, DIFF_HEADS, DIFF_V_DIM), 1.0)
    cache_k_fox = nrm((N_ODD, n_phys, PAGE_SIZE, FOX_HEADS, FOX_HEAD_DIM), 1.0)
    cache_v_fox = nrm((N_ODD, n_phys, PAGE_SIZE, FOX_HEADS, FOX_HEAD_DIM), 1.0)
    cache_logf_fox = jax.nn.log_sigmoid(nrm((N_ODD, n_phys, PAGE_SIZE, FOX_HEADS), 0.5) + FORGET_BIAS_INIT)
    state_conv_a = nrm((N_EVEN, DEC_BATCH, CONV_W - 1, A_WIDTH), 1.0)
    state_conv_ffn = nrm((DEPTH, DEC_BATCH, CONV_W - 1, D_FF), 1.0)

    norm_mix_g = 1.0 + nrm((DEPTH, D_MODEL), 0.05)
    norm_ffn_g = 1.0 + nrm((DEPTH, D_MODEL), 0.05)
    norm_final_g = 1.0 + nrm((D_MODEL,), 0.05)
    w_in_even = nrm((N_EVEN, D_MODEL, EVEN_IN), D_MODEL ** -0.5)
    w_out_even = nrm((N_EVEN, A_WIDTH + B_WIDTH, D_MODEL), (A_WIDTH + B_WIDTH) ** -0.5)
    conv_a_w = nrm((N_EVEN, CONV_W, A_WIDTH), CONV_W ** -0.5)
    lam_q1 = nrm((N_EVEN, DIFF_QK_DIM), 0.1)
    lam_k1 = nrm((N_EVEN, DIFF_QK_DIM), 0.1)
    lam_q2 = nrm((N_EVEN, DIFF_QK_DIM), 0.1)
    lam_k2 = nrm((N_EVEN, DIFF_QK_DIM), 0.1)
    subln_g = 1.0 + nrm((N_EVEN, DIFF_V_DIM), 0.05)
    rel_bias = nrm((NUM_BUCKETS, DIFF_HEADS), 0.5)
    w_in_odd = nrm((N_ODD, D_MODEL, ODD_IN), D_MODEL ** -0.5)
    w_in_odd = w_in_odd.at[..., 3 * FOX_WIDTH:].multiply(0.1)
    b_forget = FORGET_BIAS_INIT + nrm((N_ODD, FOX_HEADS), 0.5)
    w_out_odd = nrm((N_ODD, FOX_WIDTH, D_MODEL), FOX_WIDTH ** -0.5)
    w_gate_ffn = nrm((DEPTH, D_MODEL, D_FF), D_MODEL ** -0.5)
    w_up_ffn = nrm((DEPTH, D_MODEL, D_FF), D_MODEL ** -0.5)
    conv_ffn_w = nrm((DEPTH, CONV_W, D_FF), CONV_W ** -0.5)
    w_down_ffn = nrm((DEPTH, D_FF, D_MODEL), D_FF ** -0.5)
    return {
        'x_prompt': x_prompt, 'x_sample': x_sample,
        'cache_k_diff': cache_k_diff, 'cache_v_diff': cache_v_diff,
        'cache_k_fox': cache_k_fox, 'cache_v_fox': cache_v_fox, 'cache_logf_fox': cache_logf_fox,
        'state_conv_a': state_conv_a, 'state_conv_ffn': state_conv_ffn, 'page_table': page_table,
        'norm_mix_g': norm_mix_g, 'norm_ffn_g': norm_ffn_g, 'norm_final_g': norm_final_g,
        'w_in_even': w_in_even, 'w_out_even': w_out_even, 'conv_a_w': conv_a_w,
        'lam_q1': lam_q1, 'lam_k1': lam_k1, 'lam_q2': lam_q2, 'lam_k2': lam_k2,
        'subln_g': subln_g, 'rel_bias': rel_bias,
        'w_in_odd': w_in_odd, 'b_forget': b_forget, 'w_out_odd': w_out_odd,
        'w_gate_ffn': w_gate_ffn, 'w_up_ffn': w_up_ffn, 'conv_ffn_w': conv_ffn_w, 'w_down_ffn': w_down_ffn,
    }


def reference(x_prompt, x_sample, cache_k_diff, cache_v_diff, cache_k_fox, cache_v_fox, cache_logf_fox,
              state_conv_a, state_conv_ffn, page_table,
              norm_mix_g, norm_ffn_g, norm_final_g, w_in_even, w_out_even, conv_a_w,
              lam_q1, lam_k1, lam_q2, lam_k2, subln_g, rel_bias,
              w_in_odd, b_forget, w_out_odd, w_gate_ffn, w_up_ffn, conv_ffn_w, w_down_ffn):
    weights = (norm_mix_g, norm_ffn_g, norm_final_g, w_in_even, w_out_even, conv_a_w,
               lam_q1, lam_k1, lam_q2, lam_k2, subln_g, rel_bias,
               w_in_odd, b_forget, w_out_odd, w_gate_ffn, w_up_ffn, conv_ffn_w, w_down_ffn)
    y_prompt, st_p = _run_group(x_prompt, None, weights)
    past = (cache_k_diff, cache_v_diff, cache_k_fox, cache_v_fox, cache_logf_fox,
            state_conv_a, state_conv_ffn, page_table)
    y_sample, st_s = _run_group(x_sample, past, weights)
    k_diff_p, v_diff_p, k_fox_p, v_fox_p, logf_fox_p, conv_a_p, conv_ffn_p = st_p
    k_diff_s, v_diff_s, k_fox_s, v_fox_s, logf_fox_s, conv_a_s, conv_ffn_s = st_s
    return (y_prompt, y_sample,
            k_diff_p, v_diff_p, k_fox_p, v_fox_p, logf_fox_p, conv_a_p, conv_ffn_p,
            k_diff_s, v_diff_s, k_fox_s, v_fox_s, logf_fox_s, conv_a_s, conv_ffn_s)
```

```python
import functools
import math

import jax
import jax.numpy as jnp
from jax import lax
from jax.experimental import pallas as pl
from jax.experimental.pallas import tpu as pltpu

D_MODEL = 1024
DEPTH = 4
PAGE_SIZE = 128
CONV_W = 3
A_WIDTH = D_MODEL // 2
DIFF_HEADS = 4
DIFF_QK_DIM = D_MODEL // 16
DIFF_V_DIM = 2 * DIFF_QK_DIM
HEAD_W = 128
DIFF_QK_WIDTH = DIFF_HEADS * 2 * DIFF_QK_DIM
B_WIDTH = DIFF_HEADS * DIFF_V_DIM
EVEN_IN = 3 * A_WIDTH + 2 * DIFF_QK_WIDTH + B_WIDTH
FOX_HEADS = 8
FOX_HEAD_DIM = D_MODEL // FOX_HEADS
FOX_WIDTH = FOX_HEADS * FOX_HEAD_DIM
ODD_IN_PAD = 3 * FOX_WIDTH + 128
NUM_BUCKETS = 32
MAX_DISTANCE = 128
D_FF = (11 * D_MODEL) // 4
EPS = 1e-6
NEG = -1e30

F32 = jnp.float32
BF16 = jnp.bfloat16
MIB = 1024 * 1024
NT_DIMS = (((1,), (1,)), ((), ()))


def _cparams(n_axes, vmem_mib):
    return pltpu.CompilerParams(dimension_semantics=("arbitrary",) * n_axes,
                                vmem_limit_bytes=vmem_mib * MIB)


def _rms(x, g):
    ms = jnp.mean(x * x, axis=-1, keepdims=True)
    return x * lax.rsqrt(ms + EPS) * g


def _halo_rows(shift):
    return -(-(2 * shift) // 8) * 8


def _proj_kernel(x_ref, g_ref, w_ref, z_ref, zb_ref, h_ref):
    @pl.when(pl.program_id(1) == 0)
    def _():
        h_ref[...] = _rms(x_ref[...], g_ref[...]).astype(BF16)

    z = jnp.dot(h_ref[...], w_ref[...], preferred_element_type=F32)
    z_ref[...] = z
    zb_ref[...] = z.astype(BF16)


def _proj(x, g, w, tm, tn):
    m, d = x.shape
    n = w.shape[1]
    return pl.pallas_call(
        _proj_kernel,
        grid=(m // tm, n // tn),
        in_specs=[pl.BlockSpec((tm, d), lambda i, j: (i, 0)),
                  pl.BlockSpec((1, d), lambda i, j: (0, 0)),
                  pl.BlockSpec((d, tn), lambda i, j: (0, j))],
        out_specs=[pl.BlockSpec((tm, tn), lambda i, j: (i, j)),
                   pl.BlockSpec((tm, tn), lambda i, j: (i, j))],
        out_shape=[jax.ShapeDtypeStruct((m, n), F32), jax.ShapeDtypeStruct((m, n), BF16)],
        scratch_shapes=[pltpu.VMEM((tm, d), BF16)],
        compiler_params=_cparams(2, 40),
        name="proj",
    )(x, g, w)


def _softmax_step(s, v, m_ref, l_ref, acc_ref, idx):
    m_old = m_ref[idx]
    m_new = jnp.maximum(m_old, jnp.max(s, axis=-1, keepdims=True))
    alpha = jnp.exp(m_old - m_new)
    p = jnp.exp(s - m_new)
    l_ref[idx] = alpha * l_ref[idx] + jnp.sum(p, axis=-1, keepdims=True)
    acc_ref[idx] = alpha * acc_ref[idx] + jnp.dot(p.astype(BF16), v, preferred_element_type=F32)
    m_ref[idx] = m_new


def _diff_lambda(lq1_ref, lk1_ref, lq2_ref, lk2_ref, lam_init):
    return (jnp.exp(jnp.sum(lq1_ref[...] * lk1_ref[...], axis=-1, keepdims=True))
            - jnp.exp(jnp.sum(lq2_ref[...] * lk2_ref[...], axis=-1, keepdims=True)) + lam_init)


def _diff_attn_kernel(q_ref, k_ref, v_ref, bias_ref, lq1_ref, lk1_ref, lq2_ref, lk2_ref, sg_ref,
                      o_ref, m_ref, l_ref, acc_ref, *, tq, lam_init):
    qi = pl.program_id(2)
    lane = lax.broadcasted_iota(jnp.int32, (tq, HEAD_W), 1)
    qs = q_ref[...].astype(F32) * (DIFF_QK_DIM ** -0.5)
    q_maps = (jnp.where(lane < DIFF_QK_DIM, qs, 0.0).astype(BF16),
              jnp.where(lane >= DIFF_QK_DIM, qs, 0.0).astype(BF16))
    m_ref[...] = jnp.full(m_ref.shape, NEG, F32)
    l_ref[...] = jnp.zeros(l_ref.shape, F32)
    acc_ref[...] = jnp.zeros(acc_ref.shape, F32)

    def step(kt, kind):
        rows = pl.ds(pl.multiple_of(kt * tq, tq), tq)
        k = k_ref[rows, :]
        v = v_ref[rows, :]
        bias = bias_ref[kind]
        for mi in range(2):
            s = lax.dot_general(q_maps[mi], k, NT_DIMS, preferred_element_type=F32) + bias
            _softmax_step(s, v, m_ref, l_ref, acc_ref, mi)

    def far_body(kt, carry):
        step(kt, 2)
        return carry

    lax.fori_loop(0, jnp.maximum(qi - 1, 0), far_body, 0)

    @pl.when(qi >= 1)
    def _():
        step(qi - 1, 1)

    step(qi, 0)

    lam = _diff_lambda(lq1_ref, lk1_ref, lq2_ref, lk2_ref, lam_init)
    o = acc_ref[0] / l_ref[0] - lam * (acc_ref[1] / l_ref[1])
    o_ref[...] = (_rms(o, sg_ref[...]) * (1.0 - lam_init)).astype(BF16)


def _diff_attn(zb, bias, lam_params, sg, n_seq, t, tq, lam_init):
    m = zb.shape[0]
    nq = t // tq
    q_col = (3 * A_WIDTH) // HEAD_W
    k_col = q_col + DIFF_HEADS
    v_col = k_col + DIFF_HEADS
    small = pl.BlockSpec((1, DIFF_QK_DIM), lambda b, h, i: (0, 0))
    return pl.pallas_call(
        functools.partial(_diff_attn_kernel, tq=tq, lam_init=lam_init),
        grid=(n_seq, DIFF_HEADS, nq),
        in_specs=[pl.BlockSpec((tq, HEAD_W), lambda b, h, i: (b * nq + i, q_col + h)),
                  pl.BlockSpec((t, HEAD_W), lambda b, h, i: (b, k_col + h)),
                  pl.BlockSpec((t, HEAD_W), lambda b, h, i: (b, v_col + h)),
                  pl.BlockSpec((None, 3, tq, tq), lambda b, h, i: (h, 0, 0, 0)),
                  small, small, small, small,
                  pl.BlockSpec((1, DIFF_V_DIM), lambda b, h, i: (0, 0))],
        out_specs=pl.BlockSpec((tq, HEAD_W), lambda b, h, i: (b * nq + i, h)),
        out_shape=jax.ShapeDtypeStruct((m, B_WIDTH), BF16),
        scratch_shapes=[pltpu.VMEM((2, tq, 1), F32), pltpu.VMEM((2, tq, 1), F32),
                        pltpu.VMEM((2, tq, HEAD_W), F32)],
        compiler_params=_cparams(3, 40),
        name="diff_attn",
    )(zb, zb, zb, bias, *lam_params, sg)


def _fox_attn_kernel(q_ref, k_ref, v_ref, ck_ref, o_ref, m_ref, l_ref, acc_ref, *, tq):
    qi = pl.program_id(2)
    q = q_ref[...]
    scale = FOX_HEAD_DIM ** -0.5
    row = lax.broadcasted_iota(jnp.int32, (tq, tq), 0)
    col = lax.broadcasted_iota(jnp.int32, (tq, tq), 1)
    cq = jnp.sum(jnp.where(row == col, ck_ref[qi], 0.0), axis=-1, keepdims=True)
    m_ref[...] = jnp.full(m_ref.shape, NEG, F32)
    l_ref[...] = jnp.zeros(l_ref.shape, F32)
    acc_ref[...] = jnp.zeros(acc_ref.shape, F32)

    def step(kt, diagonal):
        rows = pl.ds(pl.multiple_of(kt * tq, tq), tq)
        k = k_ref[rows, :]
        v = v_ref[rows, :]
        s = lax.dot_general(q, k, NT_DIMS, preferred_element_type=F32) * scale + (cq - ck_ref[kt])
        if diagonal:
            s = jnp.where(row >= col, s, NEG)
        _softmax_step(s, v, m_ref, l_ref, acc_ref, 0)

    def far_body(kt, carry):
        step(kt, False)
        return carry

    lax.fori_loop(0, qi, far_body, 0)
    step(qi, True)
    o_ref[...] = (acc_ref[0] / l_ref[0]).astype(BF16)


def _fox_attn(zb, c_row, n_seq, t, tq):
    m = zb.shape[0]
    nq = t // tq
    return pl.pallas_call(
        functools.partial(_fox_attn_kernel, tq=tq),
        grid=(n_seq, FOX_HEADS, nq),
        in_specs=[pl.BlockSpec((tq, HEAD_W), lambda b, h, i: (b * nq + i, h)),
                  pl.BlockSpec((t, HEAD_W), lambda b, h, i: (b, FOX_HEADS + h)),
                  pl.BlockSpec((t, HEAD_W), lambda b, h, i: (b, 2 * FOX_HEADS + h)),
                  pl.BlockSpec((None, nq, 1, tq), lambda b, h, i: (b * FOX_HEADS + h, 0, 0, 0))],
        out_specs=pl.BlockSpec((tq, HEAD_W), lambda b, h, i: (b * nq + i, h)),
        out_shape=jax.ShapeDtypeStruct((m, FOX_WIDTH), BF16),
        scratch_shapes=[pltpu.VMEM((1, tq, 1), F32), pltpu.VMEM((1, tq, 1), F32),
                        pltpu.VMEM((1, tq, HEAD_W), F32)],
        compiler_params=_cparams(3, 40),
        name="fox_attn",
    )(zb, zb, zb, c_row)


def _log_sigmoid(x):
    return -(jnp.maximum(-x, 0.0) + jnp.log1p(jnp.exp(-jnp.abs(x))))


def _split3(x):
    hi = x.astype(BF16)
    r1 = x - hi.astype(F32)
    mid = r1.astype(BF16)
    lo = (r1 - mid.astype(F32)).astype(BF16)
    return hi, mid, lo


def _gates_prompt_kernel(f_ref, b_ref, logf_ref, c_ref, *, chunk):
    rows, t = f_ref.shape
    r = lax.broadcasted_iota(jnp.int32, (chunk, chunk), 0)
    cidx = lax.broadcasted_iota(jnp.int32, (chunk, chunk), 1)
    tri = jnp.where(r <= cidx, 1.0, 0.0).astype(BF16)
    carry = jnp.zeros((rows, 1), F32)
    for ci in range(t // chunk):
        sl = slice(ci * chunk, (ci + 1) * chunk)
        logf = _log_sigmoid(f_ref[:, sl] + b_ref[...])
        logf_ref[:, sl] = logf
        y = carry
        for part in _split3(logf):
            y = y + jnp.dot(part, tri, preferred_element_type=F32)
        c_ref[:, sl] = y
        carry = y[:, chunk - 1:chunk]


def _gates_prompt(f_t, b_col):
    rows, t = f_t.shape
    full = pl.BlockSpec((rows, t), lambda i: (0, 0))
    return pl.pallas_call(
        functools.partial(_gates_prompt_kernel, chunk=256),
        grid=(1,),
        in_specs=[full, pl.BlockSpec((rows, 1), lambda i: (0, 0))],
        out_specs=[full, full],
        out_shape=[jax.ShapeDtypeStruct((rows, t), F32)] * 2,
        compiler_params=_cparams(1, 32),
        name="gates_prompt",
    )(f_t, b_col)


def _gates_sample_kernel(f_ref, b_ref, logf_ref, c_ref, *, n_seq, t):
    logf = _log_sigmoid(f_ref[...] + b_ref[...])
    logf_ref[...] = logf
    run = logf[0:n_seq]
    c_ref[0:n_seq, :] = run
    for ti in range(1, t):
        run = run + logf[ti * n_seq:(ti + 1) * n_seq]
        c_ref[ti * n_seq:(ti + 1) * n_seq, :] = run


def _gates_sample(f, b_row, n_seq, t):
    m, h = f.shape
    full = pl.BlockSpec((m, h), lambda i: (0, 0))
    return pl.pallas_call(
        functools.partial(_gates_sample_kernel, n_seq=n_seq, t=t),
        grid=(1,),
        in_specs=[full, pl.BlockSpec((1, h), lambda i: (0, 0))],
        out_specs=[full, full],
        out_shape=[jax.ShapeDtypeStruct((m, h), F32)] * 2,
        compiler_params=_cparams(1, 32),
        name="gates_sample",
    )(f, b_row)


def _conv3(buf_ref, cw_ref, cur, tm, shift, halo):
    y = buf_ref[pl.ds(halo - 2 * shift, tm), :] * cw_ref[0:1, :]
    y = y + buf_ref[pl.ds(halo - shift, tm), :] * cw_ref[1:2, :]
    return y + cur * cw_ref[2:3, :]


def _mix_even_kernel(x_ref, ab_ref, ac_ref, ah_ref, ob_ref, cw_ref, prev_ref, w_ref,
                     xo_ref, st_ref, ubuf, *, tm, tps, shift):
    halo = _halo_rows(shift)
    first = (pl.program_id(0) % tps) == 0

    @pl.when(first)
    def _():
        ubuf[pl.ds(halo - 2 * shift, 2 * shift), :] = prev_ref[...]

    @pl.when(jnp.logical_not(first))
    def _():
        ubuf[pl.ds(halo - 2 * shift, 2 * shift), :] = ubuf[pl.ds(halo + tm - 2 * shift, 2 * shift), :]

    u = ac_ref[...] * ah_ref[...]
    ubuf[pl.ds(halo, tm), :] = u
    ya = ab_ref[...] * _conv3(ubuf, cw_ref, u, tm, shift, halo)
    st_ref[...] = ubuf[pl.ds(halo + tm - 2 * shift, 2 * shift), :]
    y = jnp.dot(ya.astype(BF16), w_ref[0:A_WIDTH, :], preferred_element_type=F32)
    y = y + jnp.dot(ob_ref[...], w_ref[A_WIDTH:, :], preferred_element_type=F32)
    xo_ref[...] = x_ref[...] + y


def _mix_even(x, z, ob, cw, prev, w, tm, tps, shift):
    m, d = x.shape
    groups = prev.shape[0]
    return pl.pallas_call(
        functools.partial(_mix_even_kernel, tm=tm, tps=tps, shift=shift),
        grid=(m // tm,),
        in_specs=[pl.BlockSpec((tm, d), lambda i: (i, 0)),
                  pl.BlockSpec((tm, A_WIDTH), lambda i: (i, 0)),
                  pl.BlockSpec((tm, A_WIDTH), lambda i: (i, 1)),
                  pl.BlockSpec((tm, A_WIDTH), lambda i: (i, 2)),
                  pl.BlockSpec((tm, B_WIDTH), lambda i: (i, 0)),
                  pl.BlockSpec((CONV_W, A_WIDTH), lambda i: (0, 0)),
                  pl.BlockSpec((None, 2 * shift, A_WIDTH), lambda i: (i // tps, 0, 0)),
                  pl.BlockSpec((A_WIDTH + B_WIDTH, d), lambda i: (0, 0))],
        out_specs=[pl.BlockSpec((tm, d), lambda i: (i, 0)),
                   pl.BlockSpec((None, 2 * shift, A_WIDTH), lambda i: (i // tps, 0, 0))],
        out_shape=[jax.ShapeDtypeStruct((m, d), F32),
                   jax.ShapeDtypeStruct((groups, 2 * shift, A_WIDTH), F32)],
        scratch_shapes=[pltpu.VMEM((_halo_rows(shift) + tm, A_WIDTH), F32)],
        compiler_params=_cparams(1, 48),
        name="mix_even",
    )(x, z, z, z, ob, cw, prev, w)


def _mix_odd_kernel(x_ref, o_ref, w_ref, xo_ref):
    xo_ref[...] = x_ref[...] + jnp.dot(o_ref[...], w_ref[...], preferred_element_type=F32)


def _mix_odd(x, o, w, tm):
    m, d = x.shape
    return pl.pallas_call(
        _mix_odd_kernel,
        grid=(m // tm,),
        in_specs=[pl.BlockSpec((tm, d), lambda i: (i, 0)),
                  pl.BlockSpec((tm, FOX_WIDTH), lambda i: (i, 0)),
                  pl.BlockSpec((FOX_WIDTH, d), lambda i: (0, 0))],
        out_specs=pl.BlockSpec((tm, d), lambda i: (i, 0)),
        out_shape=jax.ShapeDtypeStruct((m, d), F32),
        compiler_params=_cparams(1, 40),
        name="mix_odd",
    )(x, o, w)


def _ffn_kernel(x_ref, g_ref, wg_ref, wu_ref, cw_ref, prev_ref, wd_ref, gf_ref,
                xo_ref, st_ref, h_ref, acc_ref, gbuf, carry_ref, *, tm, tps, shift, final_norm):
    halo = _halo_rows(shift)
    i = pl.program_id(0)
    j = pl.program_id(1)
    first = (i % tps) == 0

    @pl.when(j == 0)
    def _():
        h_ref[...] = _rms(x_ref[...], g_ref[...]).astype(BF16)
        acc_ref[...] = jnp.zeros(acc_ref.shape, F32)

    h = h_ref[...]
    gate = jnp.dot(h, wg_ref[...], preferred_element_type=F32)
    up = jnp.dot(h, wu_ref[...], preferred_element_type=F32)

    @pl.when(first)
    def _():
        gbuf[pl.ds(halo - 2 * shift, 2 * shift), :] = prev_ref[...]

    @pl.when(jnp.logical_not(first))
    def _():
        gbuf[pl.ds(halo - 2 * shift, 2 * shift), :] = carry_ref[j]

    gbuf[pl.ds(halo, tm), :] = gate
    tail = gbuf[pl.ds(halo + tm - 2 * shift, 2 * shift), :]
    carry_ref[j] = tail
    st_ref[...] = tail
    gc = _conv3(gbuf, cw_ref, gate, tm, shift, halo)
    act = (gc / (1.0 + jnp.exp(-gc))) * up
    acc_ref[...] += jnp.dot(act.astype(BF16), wd_ref[...], preferred_element_type=F32)

    @pl.when(j == pl.num_programs(1) - 1)
    def _():
        xn = x_ref[...] + acc_ref[...]
        xo_ref[...] = _rms(xn, gf_ref[...]) if final_norm else xn


def _ffn(x, g, wg, wu, cw, prev, wd, gf, tm, tf, tps, shift, final_norm):
    m, d = x.shape
    f = wg.shape[1]
    groups = prev.shape[0]
    return pl.pallas_call(
        functools.partial(_ffn_kernel, tm=tm, tps=tps, shift=shift, final_norm=final_norm),
        grid=(m // tm, f // tf),
        in_specs=[pl.BlockSpec((tm, d), lambda i, j: (i, 0)),
                  pl.BlockSpec((1, d), lambda i, j: (0, 0)),
                  pl.BlockSpec((d, tf), lambda i, j: (0, j)),
                  pl.BlockSpec((d, tf), lambda i, j: (0, j)),
                  pl.BlockSpec((CONV_W, tf), lambda i, j: (0, j)),
                  pl.BlockSpec((None, 2 * shift, tf), lambda i, j: (i // tps, 0, j)),
                  pl.BlockSpec((tf, d), lambda i, j: (j, 0)),
                  pl.BlockSpec((1, d), lambda i, j: (0, 0))],
        out_specs=[pl.BlockSpec((tm, d), lambda i, j: (i, 0)),
                   pl.BlockSpec((None, 2 * shift, tf), lambda i, j: (i // tps, 0, j))],
        out_shape=[jax.ShapeDtypeStruct((m, d), F32),
                   jax.ShapeDtypeStruct((groups, 2 * shift, f), F32)],
        scratch_shapes=[pltpu.VMEM((tm, d), BF16), pltpu.VMEM((tm, d), F32),
                        pltpu.VMEM((_halo_rows(shift) + tm, tf), F32),
                        pltpu.VMEM((f // tf, 2 * shift, tf), F32)],
        compiler_params=_cparams(2, 56),
        name="ffn",
    )(x, g, wg, wu, cw, prev, wd, gf)


def _online_update(s, pv_fn, m_ref, l_ref, acc_ref):
    m_old = m_ref[...]
    m_new = jnp.maximum(m_old, jnp.max(s, axis=-1, keepdims=True))
    alpha = jnp.exp(m_old - m_new)
    p = jnp.exp(s - m_new)
    l_ref[...] = alpha * l_ref[...] + jnp.sum(p, axis=-1, keepdims=True)
    acc_ref[...] = alpha * acc_ref[...] + pv_fn(p.astype(BF16))
    m_ref[...] = m_new


def _dec_diff_kernel(pt_ref, q_ref, kn_ref, vn_ref, bfar_ref, blast_ref, bnew_ref,
                     lq1_ref, lk1_ref, lq2_ref, lk2_ref, sg_ref, *rest, n_pages_step, lam_init):
    del pt_ref
    g_n = n_pages_step
    kp, vp = rest[:g_n], rest[g_n:2 * g_n]
    o_ref, m_ref, l_ref, acc_ref = rest[2 * g_n:]
    j = pl.program_id(1)
    last = pl.num_programs(1) - 1
    rows = q_ref.shape[0]
    krows = kp[0].shape[0]

    @pl.when(j == 0)
    def _():
        m_ref[...] = jnp.full(m_ref.shape, NEG, F32)
        l_ref[...] = jnp.zeros(l_ref.shape, F32)
        acc_ref[...] = jnp.zeros(acc_ref.shape, F32)

    lane = lax.broadcasted_iota(jnp.int32, (rows, HEAD_W), 1)
    row = lax.broadcasted_iota(jnp.int32, (rows, HEAD_W), 0)
    keep = (lane // DIFF_QK_DIM) == (row // (rows // 2))
    q = jnp.where(keep, q_ref[...] * (DIFF_QK_DIM ** -0.5), 0.0).astype(BF16)

    scores = []
    for g in range(g_n):
        s = lax.dot_general(q, kp[g][...].astype(BF16), NT_DIMS, preferred_element_type=F32)
        if g == g_n - 1:
            bias = jnp.where(j == last, blast_ref[...], bfar_ref[...])
        else:
            bias = bfar_ref[...]
        scores.append(s + bias)

    def pv_pages(p):
        out = jnp.dot(p[:, 0:krows], vp[0][...].astype(BF16), preferred_element_type=F32)
        for g in range(1, g_n):
            out = out + jnp.dot(p[:, g * krows:(g + 1) * krows], vp[g][...].astype(BF16),
                                preferred_element_type=F32)
        return out

    _online_update(jnp.concatenate(scores, axis=1), pv_pages, m_ref, l_ref, acc_ref)

    @pl.when(j == last)
    def _():
        s = lax.dot_general(q, kn_ref[...].astype(BF16), NT_DIMS, preferred_element_type=F32)
        vn = vn_ref[...].astype(BF16)
        _online_update(s + bnew_ref[...], lambda p: jnp.dot(p, vn, preferred_element_type=F32),
                       m_ref, l_ref, acc_ref)
        a = acc_ref[...] / l_ref[...]
        lam = _diff_lambda(lq1_ref, lk1_ref, lq2_ref, lk2_ref, lam_init)
        o = a[0:rows // 2] - lam * a[rows // 2:rows]
        o_ref[...] = _rms(o, sg_ref[...]) * (1.0 - lam_init)


def _dec_diff(page_table, q, kn, vn, bfar, blast, bnew, lam_params, sg, cache_k, cache_v, layer,
              n_pages_step, lam_init):
    n_seq, rows, _ = q.shape
    n_pages = page_table.shape[1]
    g_n = n_pages_step
    krows = cache_k.shape[2]

    def page_spec(g):
        return pl.BlockSpec((None, None, krows, HEAD_W),
                            lambda b, j, pt: (layer, pt[b, j * g_n + g], 0, 0))

    const2 = lambda b, j, pt: (0, 0)
    per_seq = lambda b, j, pt: (b, 0, 0)
    small = pl.BlockSpec((1, DIFF_QK_DIM), const2)
    grid_spec = pltpu.PrefetchScalarGridSpec(
        num_scalar_prefetch=1,
        grid=(n_seq, n_pages // g_n),
        in_specs=[pl.BlockSpec((None, rows, HEAD_W), per_seq),
                  pl.BlockSpec((None,) + kn.shape[1:], per_seq),
                  pl.BlockSpec((None,) + vn.shape[1:], per_seq),
                  pl.BlockSpec(bfar.shape, const2),
                  pl.BlockSpec(blast.shape, const2),
                  pl.BlockSpec(bnew.shape, const2),
                  small, small, small, small,
                  pl.BlockSpec((1, DIFF_V_DIM), const2)]
                 + [page_spec(g) for g in range(g_n)] * 2,
        out_specs=pl.BlockSpec((None, rows // 2, HEAD_W), per_seq),
        scratch_shapes=[pltpu.VMEM((rows, 1), F32), pltpu.VMEM((rows, 1), F32),
                        pltpu.VMEM((rows, HEAD_W), F32)],
    )
    return pl.pallas_call(
        functools.partial(_dec_diff_kernel, n_pages_step=g_n, lam_init=lam_init),
        grid_spec=grid_spec,
        out_shape=jax.ShapeDtypeStruct((n_seq, rows // 2, HEAD_W), F32),
        compiler_params=_cparams(2, 48),
        name="dec_diff",
    )(page_table, q, kn, vn, bfar, blast, bnew, *lam_params, sg,
      *([cache_k] * g_n), *([cache_v] * g_n))


def _suffix_by_head(x, n_heads):
    width = x.shape[1]
    lane = lax.broadcasted_iota(jnp.int32, x.shape, 1)
    incl = x
    tot = x
    sh = n_heads
    while sh < width:
        shifted = pltpu.roll(incl, width - sh, axis=1)
        incl = incl + jnp.where(lane < width - sh, shifted, 0.0)
        tot = tot + pltpu.roll(tot, sh, axis=1)
        sh *= 2
    return incl, tot


def _dec_fox_kernel(pt_ref, q_ref, kn_ref, vn_ref, cncol_ref, cnrow_ref, mfar_ref, mnew_ref,
                    *rest, n_pages_step):
    del pt_ref
    g_n = n_pages_step
    kp, vp, lp = rest[:g_n], rest[g_n:2 * g_n], rest[2 * g_n:3 * g_n]
    o_ref, m_ref, l_ref, acc_ref, later_ref = rest[3 * g_n:]
    j = pl.program_id(1)
    last = pl.num_programs(1) - 1
    krows = kp[0].shape[0]
    scale = FOX_HEAD_DIM ** -0.5

    @pl.when(j == 0)
    def _():
        m_ref[...] = jnp.full(m_ref.shape, NEG, F32)
        l_ref[...] = jnp.zeros(l_ref.shape, F32)
        acc_ref[...] = jnp.zeros(acc_ref.shape, F32)
        later_ref[...] = jnp.zeros(later_ref.shape, F32)

    q = q_ref[...].astype(BF16)
    cn_col = cncol_ref[...]
    mfar = mfar_ref[...]

    later = later_ref[...]
    scores = [None] * g_n
    for g in reversed(range(g_n)):
        logf = lp[g][...]
        incl, tot = _suffix_by_head(logf, FOX_HEADS)
        decay = (incl - logf) + later
        later = later + tot
        s = lax.dot_general(q, kp[g][...].astype(BF16), NT_DIMS, preferred_element_type=F32)
        scores[g] = s * scale + (cn_col + decay) + mfar
    later_ref[...] = later

    def pv_pages(p):
        out = jnp.dot(p[:, 0:krows], vp[0][...].astype(BF16), preferred_element_type=F32)
        for g in range(1, g_n):
            out = out + jnp.dot(p[:, g * krows:(g + 1) * krows], vp[g][...].astype(BF16),
                                preferred_element_type=F32)
        return out

    _online_update(jnp.concatenate(scores, axis=1), pv_pages, m_ref, l_ref, acc_ref)

    @pl.when(j == last)
    def _():
        s = lax.dot_general(q, kn_ref[...].astype(BF16), NT_DIMS, preferred_element_type=F32)
        s = s * scale + (cn_col - cnrow_ref[...]) + mnew_ref[...]
        vn = vn_ref[...].astype(BF16)
        _online_update(s, lambda p: jnp.dot(p, vn, preferred_element_type=F32),
                       m_ref, l_ref, acc_ref)
        o_ref[...] = acc_ref[...] / l_ref[...]


def _dec_fox(page_table, q, kn, vn, cn_col, cn_row, mfar, mnew, cache_k, cache_v, cache_lf, layer,
             n_pages_step):
    n_seq, rows, _ = q.shape
    n_pages = page_table.shape[1]
    g_n = n_pages_step
    n_steps = n_pages // g_n
    krows = cache_k.shape[2]

    def page_of(b, j, pt, g):
        return pt[b, (n_steps - 1 - j) * g_n + g]

    def page_spec(g):
        return pl.BlockSpec((None, None, krows, HEAD_W),
                            lambda b, j, pt: (layer, page_of(b, j, pt, g), 0, 0))

    def logf_spec(g):
        return pl.BlockSpec((None, None, 1, krows),
                            lambda b, j, pt: (layer, page_of(b, j, pt, g), 0, 0))

    const2 = lambda b, j, pt: (0, 0)
    per_seq = lambda b, j, pt: (b, 0, 0)
    grid_spec = pltpu.PrefetchScalarGridSpec(
        num_scalar_prefetch=1,
        grid=(n_seq, n_steps),
        in_specs=[pl.BlockSpec((None, rows, HEAD_W), per_seq),
                  pl.BlockSpec((None,) + kn.shape[1:], per_seq),
                  pl.BlockSpec((None,) + vn.shape[1:], per_seq),
                  pl.BlockSpec((None,) + cn_col.shape[1:], per_seq),
                  pl.BlockSpec((None,) + cn_row.shape[1:], per_seq),
                  pl.BlockSpec(mfar.shape, const2),
                  pl.BlockSpec(mnew.shape, const2)]
                 + [page_spec(g) for g in range(g_n)] * 2
                 + [logf_spec(g) for g in range(g_n)],
        out_specs=pl.BlockSpec((None, rows, HEAD_W), per_seq),
        scratch_shapes=[pltpu.VMEM((rows, 1), F32), pltpu.VMEM((rows, 1), F32),
                        pltpu.VMEM((rows, HEAD_W), F32), pltpu.VMEM((1, krows), F32)],
    )
    return pl.pallas_call(
        functools.partial(_dec_fox_kernel, n_pages_step=g_n),
        grid_spec=grid_spec,
        out_shape=jax.ShapeDtypeStruct((n_seq, rows, HEAD_W), F32),
        compiler_params=_cparams(2, 56),
        name="dec_fox",
    )(page_table, q, kn, vn, cn_col, cn_row, mfar, mnew,
      *([cache_k] * g_n), *([cache_v] * g_n), *([cache_lf] * g_n))


def _t5_bucket(rel):
    n = jnp.maximum(rel, 0)
    max_exact = NUM_BUCKETS // 2
    large = max_exact + (jnp.log(jnp.maximum(n, 1).astype(F32) / max_exact)
                         / math.log(MAX_DISTANCE / max_exact) * (NUM_BUCKETS - max_exact)).astype(jnp.int32)
    large = jnp.minimum(large, NUM_BUCKETS - 1)
    return jnp.where(n < max_exact, n, large)


def _t5_bias(rel_bias, rel):
    return jnp.moveaxis(rel_bias[_t5_bucket(rel)].astype(F32), -1, 0)


def _prompt_bias_tiles(rel_bias, tq):
    assert tq >= MAX_DISTANCE
    i = jnp.arange(tq, dtype=jnp.int32)[:, None]
    j = jnp.arange(tq, dtype=jnp.int32)[None, :]
    diag = jnp.where(i >= j, _t5_bias(rel_bias, i - j), NEG)
    sub = _t5_bias(rel_bias, tq + i - j)
    far = _t5_bias(rel_bias, jnp.full((tq, tq), 2 * tq, jnp.int32))
    return jnp.stack([diag, sub, far], axis=1)


def _sample_diff_tables(rel_bias, past_len, t_new, new_pad):
    heads = DIFF_HEADS
    rows = 2 * heads * t_new
    r_head = (jnp.arange(rows) // t_new) % heads
    r_tok = jnp.arange(rows) % t_new
    cols = PAGE_SIZE * heads
    c_pos = jnp.arange(cols) // heads
    c_head = jnp.arange(cols) % heads
    match = r_head[:, None] == c_head[None, :]
    rel_last = (past_len + r_tok)[:, None] - (past_len - PAGE_SIZE + c_pos)[None, :]
    b_last = _t5_bias(rel_bias, rel_last)
    b_last = jnp.take_along_axis(b_last, r_head[None, :, None], axis=0)[0]
    b_far = rel_bias[_t5_bucket(jnp.int32(2 * MAX_DISTANCE))].astype(F32)[r_head][:, None]
    blast = jnp.where(match, b_last, NEG)
    bfar = jnp.where(match, jnp.broadcast_to(b_far, (rows, cols)), NEG)
    n_tok = jnp.arange(new_pad) // heads
    n_head = jnp.arange(new_pad) % heads
    valid = ((r_head[:, None] == n_head[None, :]) & (n_tok[None, :] <= r_tok[:, None])
             & (jnp.arange(new_pad)[None, :] < t_new * heads))
    rel_new = jnp.maximum(r_tok[:, None] - n_tok[None, :], 0)
    b_new = _t5_bias(rel_bias, rel_new)
    b_new = jnp.take_along_axis(b_new, r_head[None, :, None], axis=0)[0]
    bnew = jnp.where(valid, b_new, NEG)
    return bfar, blast, bnew


def _sample_fox_masks(t_new, new_pad):
    heads = FOX_HEADS
    rows = heads * t_new
    r_head = jnp.arange(rows) // t_new
    r_tok = jnp.arange(rows) % t_new
    c_head = jnp.arange(PAGE_SIZE * heads) % heads
    mfar = jnp.where(r_head[:, None] == c_head[None, :], 0.0, NEG).astype(F32)
    n_tok = jnp.arange(new_pad) // heads
    n_head = jnp.arange(new_pad) % heads
    valid = ((r_head[:, None] == n_head[None, :]) & (n_tok[None, :] <= r_tok[:, None])
             & (jnp.arange(new_pad)[None, :] < t_new * heads))
    mnew = jnp.where(valid, 0.0, NEG).astype(F32)
    return mfar, mnew


PROMPT_TM = 1024
PROMPT_TQ = 512
FFN_TF = 256
PAGES_PER_STEP = 8
NEW_PAD = 128


def _lam_init(layer):
    return 0.8 - 0.6 * math.exp(-0.3 * layer)


def _prep_weights(w):
    p = dict(w)
    p["w_in_even"] = w["w_in_even"].astype(BF16)
    p["w_out_even"] = w["w_out_even"].astype(BF16)
    pad = ODD_IN_PAD - w["w_in_odd"].shape[-1]
    p["w_in_odd"] = jnp.pad(w["w_in_odd"], ((0, 0), (0, 0), (0, pad))).astype(BF16)
    p["w_out_odd"] = w["w_out_odd"].astype(BF16)
    p["w_gate_ffn"] = w["w_gate_ffn"].astype(BF16)
    p["w_up_ffn"] = w["w_up_ffn"].astype(BF16)
    p["w_down_ffn"] = w["w_down_ffn"].astype(BF16)
    return p


def _run_prompt(x_in, w, bias_tiles):
    n_seq, t, d = x_in.shape
    tm, tq = PROMPT_TM, PROMPT_TQ
    tps = t // tm
    x = x_in.reshape(n_seq * t, d)
    outs = {k: [] for k in ("k_diff", "v_diff", "k_fox", "v_fox", "logf", "conv_a", "conv_ffn")}
    for l in range(DEPTH):
        g_mix = w["norm_mix_g"][l][None]
        if l % 2 == 0:
            e = l // 2
            z, zb = _proj(x, g_mix, w["w_in_even"][e], tm, 1024)
            lam_params = [w[n][e][None] for n in ("lam_q1", "lam_k1", "lam_q2", "lam_k2")]
            ob = _diff_attn(zb, bias_tiles, lam_params, w["subln_g"][e][None], n_seq, t, tq,
                            _lam_init(l))
            prev = jnp.zeros((n_seq, CONV_W - 1, A_WIDTH), F32)
            x, st = _mix_even(x, z, ob, w["conv_a_w"][e], prev, w["w_out_even"][e], tm, tps, 1)
            k0 = 3 * A_WIDTH + DIFF_QK_WIDTH
            outs["k_diff"].append(z[:, k0:k0 + DIFF_QK_WIDTH].reshape(n_seq, t, DIFF_HEADS, HEAD_W))
            outs["v_diff"].append(z[:, k0 + DIFF_QK_WIDTH:].reshape(n_seq, t, DIFF_HEADS, HEAD_W))
            outs["conv_a"].append(st)
        else:
            o = l // 2
            z, zb = _proj(x, g_mix, w["w_in_odd"][o], tm, 640)
            f_t = z[:, 3 * FOX_WIDTH:3 * FOX_WIDTH + FOX_HEADS].reshape(n_seq, t, FOX_HEADS)
            f_t = f_t.transpose(0, 2, 1).reshape(n_seq * FOX_HEADS, t)
            b_col = jnp.tile(w["b_forget"][o], n_seq)[:, None]
            logf_t, c_t = _gates_prompt(f_t, b_col)
            c_row = c_t.reshape(n_seq * FOX_HEADS, t // tq, 1, tq)
            oc = _fox_attn(zb, c_row, n_seq, t, tq)
            x = _mix_odd(x, oc, w["w_out_odd"][o], tm)
            outs["k_fox"].append(z[:, FOX_WIDTH:2 * FOX_WIDTH].reshape(n_seq, t, FOX_HEADS, HEAD_W))
            outs["v_fox"].append(z[:, 2 * FOX_WIDTH:3 * FOX_WIDTH].reshape(n_seq, t, FOX_HEADS, HEAD_W))
            outs["logf"].append(logf_t.reshape(n_seq, FOX_HEADS, t).transpose(0, 2, 1))
        prev_f = jnp.zeros((n_seq, CONV_W - 1, D_FF), F32)
        x, st = _ffn(x, w["norm_ffn_g"][l][None], w["w_gate_ffn"][l], w["w_up_ffn"][l],
                     w["conv_ffn_w"][l], prev_f, w["w_down_ffn"][l], w["norm_final_g"][None],
                     tm, FFN_TF, tps, 1, l == DEPTH - 1)
        outs["conv_ffn"].append(st)
    return x.reshape(n_seq, t, d), {k: jnp.stack(v) for k, v in outs.items()}


def _time_major_state(s):
    return s.transpose(1, 0, 2).reshape(1, -1, s.shape[-1])


def _state_from_time_major(s, n_seq):
    return s.reshape(CONV_W - 1, n_seq, s.shape[-1]).transpose(1, 0, 2)


def _run_sample(x_in, past, w):
    (cache_k_diff, cache_v_diff, cache_k_fox, cache_v_fox, cache_logf_fox,
     state_conv_a, state_conv_ffn, page_table) = past
    n_seq, t, d = x_in.shape
    m = n_seq * t
    past_len = page_table.shape[1] * PAGE_SIZE
    ck_diff = cache_k_diff.reshape(cache_k_diff.shape[:2] + (PAGE_SIZE * DIFF_HEADS, HEAD_W))
    cv_diff = cache_v_diff.reshape(cache_v_diff.shape[:2] + (PAGE_SIZE * DIFF_HEADS, HEAD_W))
    ck_fox = cache_k_fox.reshape(cache_k_fox.shape[:2] + (PAGE_SIZE * FOX_HEADS, HEAD_W))
    cv_fox = cache_v_fox.reshape(cache_v_fox.shape[:2] + (PAGE_SIZE * FOX_HEADS, HEAD_W))
    clf_fox = cache_logf_fox.reshape(cache_logf_fox.shape[:2] + (1, PAGE_SIZE * FOX_HEADS))
    bfar, blast, bnew = _sample_diff_tables(w["rel_bias"], past_len, t, NEW_PAD)
    mfar, mnew = _sample_fox_masks(t, NEW_PAD)

    def per_seq(a, heads):
        return a.reshape(t, n_seq, heads, HEAD_W).transpose(1, 0, 2, 3)

    def pad_new(a):
        a = a.reshape(n_seq, -1, HEAD_W)
        return jnp.pad(a, ((0, 0), (0, NEW_PAD - a.shape[1]), (0, 0)))

    x = x_in.transpose(1, 0, 2).reshape(m, d)
    outs = {k: [] for k in ("k_diff", "v_diff", "k_fox", "v_fox", "logf", "conv_a", "conv_ffn")}
    for l in range(DEPTH):
        g_mix = w["norm_mix_g"][l][None]
        if l % 2 == 0:
            e = l // 2
            z, _ = _proj(x, g_mix, w["w_in_even"][e], m, 1024)
            q0 = 3 * A_WIDTH
            q = per_seq(z[:, q0:q0 + DIFF_QK_WIDTH], DIFF_HEADS)
            k = per_seq(z[:, q0 + DIFF_QK_WIDTH:q0 + 2 * DIFF_QK_WIDTH], DIFF_HEADS)
            v = per_seq(z[:, q0 + 2 * DIFF_QK_WIDTH:], DIFF_HEADS)
            q_rows = q.transpose(0, 2, 1, 3).reshape(n_seq, DIFF_HEADS * t, HEAD_W)
            q_rows = jnp.concatenate([q_rows, q_rows], axis=1)
            lam_params = [w[n][e][None] for n in ("lam_q1", "lam_k1", "lam_q2", "lam_k2")]
            o_dec = _dec_diff(page_table, q_rows, pad_new(k), pad_new(v), bfar, blast, bnew,
                              lam_params, w["subln_g"][e][None], ck_diff, cv_diff, e,
                              PAGES_PER_STEP, _lam_init(l))
            ob = o_dec.reshape(n_seq, DIFF_HEADS, t, HEAD_W).transpose(2, 0, 1, 3)
            ob = ob.reshape(m, B_WIDTH).astype(BF16)
            x, st = _mix_even(x, z, ob, w["conv_a_w"][e], _time_major_state(state_conv_a[e]),
                              w["w_out_even"][e], m, 1, n_seq)
            outs["k_diff"].append(k)
            outs["v_diff"].append(v)
            outs["conv_a"].append(_state_from_time_major(st, n_seq))
        else:
            o = l // 2
            z, _ = _proj(x, g_mix, w["w_in_odd"][o], m, 640)
            q = per_seq(z[:, :FOX_WIDTH], FOX_HEADS)
            k = per_seq(z[:, FOX_WIDTH:2 * FOX_WIDTH], FOX_HEADS)
            v = per_seq(z[:, 2 * FOX_WIDTH:3 * FOX_WIDTH], FOX_HEADS)
            f = z[:, 3 * FOX_WIDTH:3 * FOX_WIDTH + FOX_HEADS]
            logf, cn = _gates_sample(f, w["b_forget"][o][None], n_seq, t)
            cn = cn.reshape(t, n_seq, FOX_HEADS)
            cn_col = cn.transpose(1, 2, 0).reshape(n_seq, FOX_HEADS * t, 1)
            cn_row = cn.transpose(1, 0, 2).reshape(n_seq, 1, t * FOX_HEADS)
            cn_row = jnp.pad(cn_row, ((0, 0), (0, 0), (0, NEW_PAD - t * FOX_HEADS)))
            q_rows = q.transpose(0, 2, 1, 3).reshape(n_seq, FOX_HEADS * t, HEAD_W)
            o_dec = _dec_fox(page_table, q_rows, pad_new(k), pad_new(v), cn_col, cn_row, mfar, mnew,
                             ck_fox, cv_fox, clf_fox, o, PAGES_PER_STEP)
            oc = o_dec.reshape(n_seq, FOX_HEADS, t, HEAD_W).transpose(2, 0, 1, 3)
            oc = oc.reshape(m, FOX_WIDTH).astype(BF16)
            x = _mix_odd(x, oc, w["w_out_odd"][o], m)
            outs["k_fox"].append(k)
            outs["v_fox"].append(v)
            outs["logf"].append(logf.reshape(t, n_seq, FOX_HEADS).transpose(1, 0, 2))
        x, st = _ffn(x, w["norm_ffn_g"][l][None], w["w_gate_ffn"][l], w["w_up_ffn"][l],
                     w["conv_ffn_w"][l], _time_major_state(state_conv_ffn[l]), w["w_down_ffn"][l],
                     w["norm_final_g"][None], m, FFN_TF, 1, n_seq, l == DEPTH - 1)
        outs["conv_ffn"].append(_state_from_time_major(st, n_seq))
    y = x.reshape(t, n_seq, d).transpose(1, 0, 2)
    return y, {k: jnp.stack(v) for k, v in outs.items()}


def kernel(x_prompt, x_sample, cache_k_diff, cache_v_diff, cache_k_fox, cache_v_fox, cache_logf_fox,
           state_conv_a, state_conv_ffn, page_table,
           norm_mix_g, norm_ffn_g, norm_final_g, w_in_even, w_out_even, conv_a_w,
           lam_q1, lam_k1, lam_q2, lam_k2, subln_g, rel_bias,
           w_in_odd, b_forget, w_out_odd, w_gate_ffn, w_up_ffn, conv_ffn_w, w_down_ffn):
    w = _prep_weights(dict(
        norm_mix_g=norm_mix_g, norm_ffn_g=norm_ffn_g, norm_final_g=norm_final_g,
        w_in_even=w_in_even, w_out_even=w_out_even, conv_a_w=conv_a_w,
        lam_q1=lam_q1, lam_k1=lam_k1, lam_q2=lam_q2, lam_k2=lam_k2, subln_g=subln_g,
        rel_bias=rel_bias, w_in_odd=w_in_odd, b_forget=b_forget, w_out_odd=w_out_odd,
        w_gate_ffn=w_gate_ffn, w_up_ffn=w_up_ffn, conv_ffn_w=conv_ffn_w, w_down_ffn=w_down_ffn))
    bias_tiles = _prompt_bias_tiles(rel_bias, PROMPT_TQ)
    y_p, sp = _run_prompt(x_prompt, w, bias_tiles)
    past = (cache_k_diff, cache_v_diff, cache_k_fox, cache_v_fox, cache_logf_fox,
            state_conv_a, state_conv_ffn, page_table)
    y_s, ss = _run_sample(x_sample, past, w)
    order = ("k_diff", "v_diff", "k_fox", "v_fox", "logf", "conv_a", "conv_ffn")
    return (y_p, y_s) + tuple(sp[k] for k in order) + tuple(ss[k] for k in order)
```

```python
import functools
import math

import jax
import jax.numpy as jnp
from jax import lax
from jax.experimental import pallas as pl
from jax.experimental.pallas import tpu as pltpu

D_MODEL = 1024
DEPTH = 4
PAGE_SIZE = 128
CONV_W = 3
A_WIDTH = D_MODEL // 2
DIFF_HEADS = 4
DIFF_QK_DIM = D_MODEL // 16
DIFF_V_DIM = 2 * DIFF_QK_DIM
HEAD_W = 128
DIFF_QK_WIDTH = DIFF_HEADS * 2 * DIFF_QK_DIM
B_WIDTH = DIFF_HEADS * DIFF_V_DIM
EVEN_IN = 3 * A_WIDTH + 2 * DIFF_QK_WIDTH + B_WIDTH
FOX_HEADS = 8
FOX_HEAD_DIM = D_MODEL // FOX_HEADS
FOX_WIDTH = FOX_HEADS * FOX_HEAD_DIM
NUM_BUCKETS = 32
MAX_DISTANCE = 128
D_FF = (11 * D_MODEL) // 4
EPS = 1e-6
NEG = -1e30
LOG2E = 1.4426950408889634
FOX_HEADS_PER_STEP = 2

F32 = jnp.float32
BF16 = jnp.bfloat16
MIB = 1024 * 1024
NT_DIMS = (((1,), (1,)), ((), ()))


def _cparams(n_axes, vmem_mib):
    return pltpu.CompilerParams(dimension_semantics=("arbitrary",) * n_axes,
                                vmem_limit_bytes=vmem_mib * MIB)


def _rms(x, g):
    ms = jnp.mean(x * x, axis=-1, keepdims=True)
    return x * lax.rsqrt(ms + EPS) * g


def _halo_rows(shift):
    return -(-(2 * shift) // 8) * 8


def _proj_kernel(x_ref, g_ref, w_ref, *rest, plan, has_tail):
    n_out = len(plan)
    outs, h_ref = rest[-1 - n_out - has_tail:-1], rest[-1]
    j = pl.program_id(1)

    @pl.when(j == 0)
    def _():
        h = _rms(x_ref[...], g_ref[...]).astype(BF16)
        h_ref[...] = h
        if has_tail:
            outs[-1][...] = jnp.dot(h, rest[0][...], preferred_element_type=F32)

    z = jnp.dot(h_ref[...], w_ref[...], preferred_element_type=F32)
    for o_ref, (dtype, first, count) in zip(outs, plan):
        @pl.when(jnp.logical_and(j >= first, j < first + count))
        def _():
            o_ref[...] = z.astype(dtype)


def _proj(x, g, w, plan, tm, tn, w_tail=None):
    m, d = x.shape
    has_tail = w_tail is not None

    def out_spec(first, count):
        return pl.BlockSpec((tm, tn), lambda i, j: (i, jnp.clip(j - first, 0, count - 1)))

    in_specs = [pl.BlockSpec((tm, d), lambda i, j: (i, 0)),
                pl.BlockSpec((1, d), lambda i, j: (0, 0)),
                pl.BlockSpec((d, tn), lambda i, j: (0, j))]
    out_specs = [out_spec(first, count) for _, first, count in plan]
    out_shape = [jax.ShapeDtypeStruct((m, count * tn), dtype) for dtype, _, count in plan]
    args = [x, g, w]
    if has_tail:
        in_specs.append(pl.BlockSpec(w_tail.shape, lambda i, j: (0, 0)))
        out_specs.append(pl.BlockSpec((tm, w_tail.shape[1]), lambda i, j: (i, 0)))
        out_shape.append(jax.ShapeDtypeStruct((m, w_tail.shape[1]), F32))
        args.append(w_tail)
    return pl.pallas_call(
        functools.partial(_proj_kernel, plan=tuple(plan), has_tail=has_tail),
        grid=(m // tm, w.shape[1] // tn),
        in_specs=in_specs,
        out_specs=out_specs,
        out_shape=out_shape,
        scratch_shapes=[pltpu.VMEM((tm, d), BF16)],
        compiler_params=_cparams(2, 40),
        name="proj",
    )(*args)


def _softmax_rows(t, m_ref, l_ref, acc_ref, idx):
    m_old = m_ref[idx]
    m_new = jnp.maximum(m_old, jnp.max(t, axis=-1, keepdims=True))
    alpha = jnp.exp2(m_old - m_new)
    p = jnp.exp2(t - jnp.tile(m_new, (1, t.shape[1] // HEAD_W)))
    m_ref[idx] = m_new
    l_ref[idx] = alpha * l_ref[idx] + jnp.sum(p, axis=-1, keepdims=True)
    acc_ref[idx] = alpha * acc_ref[idx]
    return p.astype(BF16)


def _diff_lambda(lq1_ref, lk1_ref, lq2_ref, lk2_ref, lam_init):
    return (jnp.exp(jnp.sum(lq1_ref[...] * lk1_ref[...], axis=-1, keepdims=True))
            - jnp.exp(jnp.sum(lq2_ref[...] * lk2_ref[...], axis=-1, keepdims=True)) + lam_init)


def _diff_attn_kernel(q_ref, k_ref, v_ref, bias_ref, lq1_ref, lk1_ref, lq2_ref, lk2_ref, sg_ref,
                      o_ref, m_ref, l_ref, acc_ref, *, tq, lam_init):
    qi = pl.program_id(2)
    lane = lax.broadcasted_iota(jnp.int32, (tq, HEAD_W), 1)
    qs = q_ref[...].astype(F32) * (DIFF_QK_DIM ** -0.5)
    q_maps = (jnp.where(lane < DIFF_QK_DIM, qs, 0.0).astype(BF16),
              jnp.where(lane >= DIFF_QK_DIM, qs, 0.0).astype(BF16))
    m_ref[...] = jnp.full(m_ref.shape, NEG, F32)
    l_ref[...] = jnp.zeros(l_ref.shape, F32)
    acc_ref[...] = jnp.zeros(acc_ref.shape, F32)

    def step(kt, kind):
        rows = pl.ds(pl.multiple_of(kt * tq, tq), tq)
        k = k_ref[rows, :]
        v = v_ref[rows, :]
        for mi in range(2):
            t = lax.dot_general(q_maps[mi], k, NT_DIMS, preferred_element_type=F32) * LOG2E
            if kind is not None:
                t = t + bias_ref[kind] * LOG2E
            p = _softmax_rows(t, m_ref, l_ref, acc_ref, mi)
            acc_ref[mi] += jnp.dot(p, v, preferred_element_type=F32)

    def far_body(kt, carry):
        step(kt, None)
        return carry

    lax.fori_loop(0, jnp.maximum(qi - 1, 0), far_body, 0)

    @pl.when(qi >= 1)
    def _():
        step(qi - 1, 1)

    step(qi, 0)

    lam = _diff_lambda(lq1_ref, lk1_ref, lq2_ref, lk2_ref, lam_init)
    o = acc_ref[0] / l_ref[0] - lam * (acc_ref[1] / l_ref[1])
    o_ref[...] = (_rms(o, sg_ref[...]) * (1.0 - lam_init)).astype(BF16)


def _diff_attn(zb, bias, lam_params, sg, n_seq, t, tq, lam_init):
    m = zb.shape[0]
    nq = t // tq
    q_col, k_col, v_col = 0, DIFF_HEADS, 2 * DIFF_HEADS
    small = pl.BlockSpec((1, DIFF_QK_DIM), lambda b, h, i: (0, 0))
    return pl.pallas_call(
        functools.partial(_diff_attn_kernel, tq=tq, lam_init=lam_init),
        grid=(n_seq, DIFF_HEADS, nq),
        in_specs=[pl.BlockSpec((tq, HEAD_W), lambda b, h, i: (b * nq + i, q_col + h)),
                  pl.BlockSpec((t, HEAD_W), lambda b, h, i: (b, k_col + h)),
                  pl.BlockSpec((t, HEAD_W), lambda b, h, i: (b, v_col + h)),
                  pl.BlockSpec((None, 2, tq, tq), lambda b, h, i: (h, 0, 0, 0)),
                  small, small, small, small,
                  pl.BlockSpec((1, DIFF_V_DIM), lambda b, h, i: (0, 0))],
        out_specs=pl.BlockSpec((tq, HEAD_W), lambda b, h, i: (b * nq + i, h)),
        out_shape=jax.ShapeDtypeStruct((m, B_WIDTH), BF16),
        scratch_shapes=[pltpu.VMEM((2, tq, HEAD_W), F32)] * 3,
        compiler_params=_cparams(3, 40),
        name="diff_attn",
    )(zb, zb, zb, bias, *lam_params, sg)


def _fox_attn_kernel(q_ref, k_ref, v_ref, ck_ref, o_ref, m_ref, l_ref, acc_ref, *, tq):
    qi = pl.program_id(2)
    scale2 = (FOX_HEAD_DIM ** -0.5) * LOG2E
    row = lax.broadcasted_iota(jnp.int32, (tq, tq), 0)
    col = lax.broadcasted_iota(jnp.int32, (tq, tq), 1)
    m_ref[...] = jnp.full(m_ref.shape, NEG, F32)
    l_ref[...] = jnp.zeros(l_ref.shape, F32)
    acc_ref[...] = jnp.zeros(acc_ref.shape, F32)
    heads = [slice(hh * HEAD_W, (hh + 1) * HEAD_W) for hh in range(FOX_HEADS_PER_STEP)]
    qs = [q_ref[:, hs] for hs in heads]
    cqs = [jnp.sum(jnp.where(row == col, ck_ref[hh, qi], 0.0), axis=-1, keepdims=True) * LOG2E
           for hh in range(FOX_HEADS_PER_STEP)]

    def step(kt, diagonal):
        rows = pl.ds(pl.multiple_of(kt * tq, tq), tq)
        for hh, hs in enumerate(heads):
            s = lax.dot_general(qs[hh], k_ref[rows, hs], NT_DIMS, preferred_element_type=F32)
            t = s * scale2 + (cqs[hh] - ck_ref[hh, kt] * LOG2E)
            if diagonal:
                t = jnp.where(row >= col, t, NEG)
            p = _softmax_rows(t, m_ref, l_ref, acc_ref, hh)
            acc_ref[hh] += jnp.dot(p, v_ref[rows, hs], preferred_element_type=F32)

    def far_body(kt, carry):
        step(kt, False)
        return carry

    lax.fori_loop(0, qi, far_body, 0)
    step(qi, True)
    for hh, hs in enumerate(heads):
        o_ref[:, hs] = (acc_ref[hh] / l_ref[hh]).astype(BF16)


def _fox_attn(zb, c_row, n_seq, t, tq):
    m = zb.shape[0]
    nq = t // tq
    hps = FOX_HEADS_PER_STEP
    groups = FOX_HEADS // hps
    width = hps * HEAD_W
    return pl.pallas_call(
        functools.partial(_fox_attn_kernel, tq=tq),
        grid=(n_seq, groups, nq),
        in_specs=[pl.BlockSpec((tq, width), lambda b, h, i: (b * nq + i, h)),
                  pl.BlockSpec((t, width), lambda b, h, i: (b, groups + h)),
                  pl.BlockSpec((t, width), lambda b, h, i: (b, 2 * groups + h)),
                  pl.BlockSpec((hps, nq, 1, tq), lambda b, h, i: (b * groups + h, 0, 0, 0))],
        out_specs=pl.BlockSpec((tq, width), lambda b, h, i: (b * nq + i, h)),
        out_shape=jax.ShapeDtypeStruct((m, FOX_WIDTH), BF16),
        scratch_shapes=[pltpu.VMEM((hps, tq, HEAD_W), F32)] * 3,
        compiler_params=_cparams(3, 40),
        name="fox_attn",
    )(zb, zb, zb, c_row)


def _log_sigmoid(x):
    return -(jnp.maximum(-x, 0.0) + jnp.log1p(jnp.exp(-jnp.abs(x))))


def _split3(x):
    hi = x.astype(BF16)
    r1 = x - hi.astype(F32)
    mid = r1.astype(BF16)
    lo = (r1 - mid.astype(F32)).astype(BF16)
    return hi, mid, lo


def _gates_prompt_kernel(f_ref, b_ref, logf_ref, c_ref, *, chunk):
    rows, t = f_ref.shape
    r = lax.broadcasted_iota(jnp.int32, (chunk, chunk), 0)
    cidx = lax.broadcasted_iota(jnp.int32, (chunk, chunk), 1)
    tri = jnp.where(r <= cidx, 1.0, 0.0).astype(BF16)
    carry = jnp.zeros((rows, 1), F32)
    for ci in range(t // chunk):
        sl = slice(ci * chunk, (ci + 1) * chunk)
        logf = _log_sigmoid(f_ref[:, sl] + b_ref[...])
        logf_ref[:, sl] = logf
        y = carry
        for part in _split3(logf):
            y = y + jnp.dot(part, tri, preferred_element_type=F32)
        c_ref[:, sl] = y
        carry = y[:, chunk - 1:chunk]


def _gates_prompt(f_t, b_col):
    rows, t = f_t.shape
    full = pl.BlockSpec((rows, t), lambda i: (0, 0))
    return pl.pallas_call(
        functools.partial(_gates_prompt_kernel, chunk=256),
        grid=(1,),
        in_specs=[full, pl.BlockSpec((rows, 1), lambda i: (0, 0))],
        out_specs=[full, full],
        out_shape=[jax.ShapeDtypeStruct((rows, t), F32)] * 2,
        compiler_params=_cparams(1, 32),
        name="gates_prompt",
    )(f_t, b_col)


def _gates_sample_kernel(f_ref, b_ref, logf_ref, c_ref, *, n_seq, t):
    logf = _log_sigmoid(f_ref[...] + b_ref[...])
    logf_ref[...] = logf
    run = logf[0:n_seq]
    c_ref[0:n_seq, :] = run
    for ti in range(1, t):
        run = run + logf[ti * n_seq:(ti + 1) * n_seq]
        c_ref[ti * n_seq:(ti + 1) * n_seq, :] = run


def _gates_sample(f, b_row, n_seq, t):
    m, h = f.shape
    full = pl.BlockSpec((m, h), lambda i: (0, 0))
    return pl.pallas_call(
        functools.partial(_gates_sample_kernel, n_seq=n_seq, t=t),
        grid=(1,),
        in_specs=[full, pl.BlockSpec((1, h), lambda i: (0, 0))],
        out_specs=[full, full],
        out_shape=[jax.ShapeDtypeStruct((m, h), F32)] * 2,
        compiler_params=_cparams(1, 32),
        name="gates_sample",
    )(f, b_row)


def _conv3(buf_ref, cw_ref, cur, tm, shift, halo):
    y = buf_ref[pl.ds(halo - 2 * shift, tm), :] * cw_ref[0:1, :]
    y = y + buf_ref[pl.ds(halo - shift, tm), :] * cw_ref[1:2, :]
    return y + cur * cw_ref[2:3, :]


def _mix_even_kernel(x_ref, ab_ref, ac_ref, ah_ref, ob_ref, cw_ref, prev_ref, w_ref,
                     xo_ref, st_ref, ubuf, *, tm, tps, shift):
    halo = _halo_rows(shift)
    first = (pl.program_id(0) % tps) == 0

    @pl.when(first)
    def _():
        ubuf[pl.ds(halo - 2 * shift, 2 * shift), :] = prev_ref[...]

    @pl.when(jnp.logical_not(first))
    def _():
        ubuf[pl.ds(halo - 2 * shift, 2 * shift), :] = ubuf[pl.ds(halo + tm - 2 * shift, 2 * shift), :]

    u = ac_ref[...] * ah_ref[...]
    ubuf[pl.ds(halo, tm), :] = u
    ya = ab_ref[...] * _conv3(ubuf, cw_ref, u, tm, shift, halo)
    st_ref[...] = ubuf[pl.ds(halo + tm - 2 * shift, 2 * shift), :]
    y = jnp.dot(ya.astype(BF16), w_ref[0:A_WIDTH, :], preferred_element_type=F32)
    y = y + jnp.dot(ob_ref[...], w_ref[A_WIDTH:, :], preferred_element_type=F32)
    xo_ref[...] = x_ref[...] + y


def _mix_even(x, z, ob, cw, prev, w, tm, tps, shift):
    m, d = x.shape
    return pl.pallas_call(
        functools.partial(_mix_even_kernel, tm=tm, tps=tps, shift=shift),
        grid=(m // tm,),
        in_specs=[pl.BlockSpec((tm, d), lambda i: (i, 0)),
                  pl.BlockSpec((tm, A_WIDTH), lambda i: (i, 0)),
                  pl.BlockSpec((tm, A_WIDTH), lambda i: (i, 1)),
                  pl.BlockSpec((tm, A_WIDTH), lambda i: (i, 2)),
                  pl.BlockSpec((tm, B_WIDTH), lambda i: (i, 0)),
                  pl.BlockSpec((CONV_W, A_WIDTH), lambda i: (0, 0)),
                  pl.BlockSpec((None, 2 * shift, A_WIDTH), lambda i: (i // tps, 0, 0)),
                  pl.BlockSpec((A_WIDTH + B_WIDTH, d), lambda i: (0, 0))],
        out_specs=[pl.BlockSpec((tm, d), lambda i: (i, 0)),
                   pl.BlockSpec((None, 2 * shift, A_WIDTH), lambda i: (i, 0, 0))],
        out_shape=[jax.ShapeDtypeStruct((m, d), F32),
                   jax.ShapeDtypeStruct((m // tm, 2 * shift, A_WIDTH), F32)],
        scratch_shapes=[pltpu.VMEM((_halo_rows(shift) + tm, A_WIDTH), F32)],
        compiler_params=_cparams(1, 48),
        name="mix_even",
    )(x, z, z, z, ob, cw, prev, w)


def _mix_odd_kernel(x_ref, o_ref, w_ref, xo_ref):
    xo_ref[...] = x_ref[...] + jnp.dot(o_ref[...], w_ref[...], preferred_element_type=F32)


def _mix_odd(x, o, w, tm):
    m, d = x.shape
    return pl.pallas_call(
        _mix_odd_kernel,
        grid=(m // tm,),
        in_specs=[pl.BlockSpec((tm, d), lambda i: (i, 0)),
                  pl.BlockSpec((tm, FOX_WIDTH), lambda i: (i, 0)),
                  pl.BlockSpec((FOX_WIDTH, d), lambda i: (0, 0))],
        out_specs=pl.BlockSpec((tm, d), lambda i: (i, 0)),
        out_shape=jax.ShapeDtypeStruct((m, d), F32),
        compiler_params=_cparams(1, 40),
        name="mix_odd",
    )(x, o, w)


def _ffn_kernel(x_ref, g_ref, wg_ref, wu_ref, cw_ref, prev_ref, wd_ref, gf_ref,
                xo_ref, st_ref, h_ref, acc_ref, gbuf, carry_ref, *, tm, tps, shift, final_norm):
    halo = _halo_rows(shift)
    i = pl.program_id(0)
    j = pl.program_id(1)
    first = (i % tps) == 0

    @pl.when(j == 0)
    def _():
        h_ref[...] = _rms(x_ref[...], g_ref[...]).astype(BF16)
        acc_ref[...] = jnp.zeros(acc_ref.shape, F32)

    @pl.when(first)
    def _():
        gbuf[pl.ds(halo - 2 * shift, 2 * shift), :] = prev_ref[...]

    @pl.when(jnp.logical_not(first))
    def _():
        gbuf[pl.ds(halo - 2 * shift, 2 * shift), :] = carry_ref[j]

    h = h_ref[...]
    gate = jnp.dot(h, wg_ref[...], preferred_element_type=F32)
    up = jnp.dot(h, wu_ref[...], preferred_element_type=F32)
    gbuf[pl.ds(halo, tm), :] = gate
    tail = gbuf[pl.ds(halo + tm - 2 * shift, 2 * shift), :]
    carry_ref[j] = tail
    st_ref[...] = tail
    gc = _conv3(gbuf, cw_ref, gate, tm, shift, halo)
    act = (gc * (0.5 * jnp.tanh(0.5 * gc) + 0.5)) * up
    acc_ref[...] += jnp.dot(act.astype(BF16), wd_ref[...], preferred_element_type=F32)

    @pl.when(j == pl.num_programs(1) - 1)
    def _():
        xn = x_ref[...] + acc_ref[...]
        xo_ref[...] = _rms(xn, gf_ref[...]) if final_norm else xn


def _ffn(x, g, wg, wu, cw, prev, wd, gf, tm, tf, tps, shift, final_norm):
    m, d = x.shape
    f = wg.shape[1]
    return pl.pallas_call(
        functools.partial(_ffn_kernel, tm=tm, tps=tps, shift=shift, final_norm=final_norm),
        grid=(m // tm, f // tf),
        in_specs=[pl.BlockSpec((tm, d), lambda i, j: (i, 0)),
                  pl.BlockSpec((1, d), lambda i, j: (0, 0)),
                  pl.BlockSpec((d, tf), lambda i, j: (0, j)),
                  pl.BlockSpec((d, tf), lambda i, j: (0, j)),
                  pl.BlockSpec((CONV_W, tf), lambda i, j: (0, j)),
                  pl.BlockSpec((None, 2 * shift, tf), lambda i, j: (i // tps, 0, j)),
                  pl.BlockSpec((tf, d), lambda i, j: (j, 0)),
                  pl.BlockSpec((1, d), lambda i, j: (0, 0))],
        out_specs=[pl.BlockSpec((tm, d), lambda i, j: (i, 0)),
                   pl.BlockSpec((None, 2 * shift, tf), lambda i, j: (i, 0, j))],
        out_shape=[jax.ShapeDtypeStruct((m, d), F32),
                   jax.ShapeDtypeStruct((m // tm, 2 * shift, f), F32)],
        scratch_shapes=[pltpu.VMEM((tm, d), BF16), pltpu.VMEM((tm, d), F32),
                        pltpu.VMEM((_halo_rows(shift) + tm, tf), F32),
                        pltpu.VMEM((f // tf, 2 * shift, tf), F32)],
        compiler_params=_cparams(2, 56),
        name="ffn",
    )(x, g, wg, wu, cw, prev, wd, gf)


def _online_update(s, pv_fn, m_ref, l_ref, acc_ref):
    m_old = m_ref[...]
    m_new = jnp.maximum(m_old, jnp.max(s, axis=-1, keepdims=True))
    alpha = jnp.exp(m_old - m_new)
    p = jnp.exp(s - m_new)
    l_ref[...] = alpha * l_ref[...] + jnp.sum(p, axis=-1, keepdims=True)
    acc_ref[...] = alpha * acc_ref[...] + pv_fn(p.astype(BF16))
    m_ref[...] = m_new


def _dec_diff_kernel(pt_ref, q_ref, kn_ref, vn_ref, bfar_ref, blast_ref, bnew_ref,
                     lq1_ref, lk1_ref, lq2_ref, lk2_ref, sg_ref, *rest, n_pages_step, lam_init):
    del pt_ref
    g_n = n_pages_step
    kp, vp = rest[:g_n], rest[g_n:2 * g_n]
    o_ref, m_ref, l_ref, acc_ref = rest[2 * g_n:]
    j = pl.program_id(1)
    last = pl.num_programs(1) - 1
    rows = q_ref.shape[0]
    krows = kp[0].shape[0]

    @pl.when(j == 0)
    def _():
        m_ref[...] = jnp.full(m_ref.shape, NEG, F32)
        l_ref[...] = jnp.zeros(l_ref.shape, F32)
        acc_ref[...] = jnp.zeros(acc_ref.shape, F32)

    lane = lax.broadcasted_iota(jnp.int32, (rows, HEAD_W), 1)
    row = lax.broadcasted_iota(jnp.int32, (rows, HEAD_W), 0)
    keep = (lane // DIFF_QK_DIM) == (row // (rows // 2))
    q = jnp.where(keep, q_ref[...].astype(F32) * (DIFF_QK_DIM ** -0.5), 0.0).astype(BF16)

    scores = []
    for g in range(g_n):
        s = lax.dot_general(q, kp[g][...].astype(BF16), NT_DIMS, preferred_element_type=F32)
        if g == g_n - 1:
            bias = jnp.where(j == last, blast_ref[...], bfar_ref[...])
        else:
            bias = bfar_ref[...]
        scores.append(s + bias)

    def pv_pages(p):
        out = jnp.dot(p[:, 0:krows], vp[0][...].astype(BF16), preferred_element_type=F32)
        for g in range(1, g_n):
            out = out + jnp.dot(p[:, g * krows:(g + 1) * krows], vp[g][...].astype(BF16),
                                preferred_element_type=F32)
        return out

    _online_update(jnp.concatenate(scores, axis=1), pv_pages, m_ref, l_ref, acc_ref)

    @pl.when(j == last)
    def _():
        s = lax.dot_general(q, kn_ref[...].astype(BF16), NT_DIMS, preferred_element_type=F32)
        vn = vn_ref[...].astype(BF16)
        _online_update(s + bnew_ref[...], lambda p: jnp.dot(p, vn, preferred_element_type=F32),
                       m_ref, l_ref, acc_ref)
        a = acc_ref[...] / l_ref[...]
        lam = _diff_lambda(lq1_ref, lk1_ref, lq2_ref, lk2_ref, lam_init)
        o = a[0:rows // 2] - lam * a[rows // 2:rows]
        o_ref[...] = _rms(o, sg_ref[...]) * (1.0 - lam_init)


def _dec_diff(page_table, q, kn, vn, bfar, blast, bnew, lam_params, sg, cache_k, cache_v, layer,
              n_pages_step, lam_init):
    n_seq, rows, _ = q.shape
    n_pages = page_table.shape[1]
    g_n = n_pages_step
    krows = cache_k.shape[2]

    def page_spec(g):
        return pl.BlockSpec((None, None, krows, HEAD_W),
                            lambda b, j, pt: (layer, pt[b, j * g_n + g], 0, 0))

    const2 = lambda b, j, pt: (0, 0)
    per_seq = lambda b, j, pt: (b, 0, 0)
    small = pl.BlockSpec((1, DIFF_QK_DIM), const2)
    grid_spec = pltpu.PrefetchScalarGridSpec(
        num_scalar_prefetch=1,
        grid=(n_seq, n_pages // g_n),
        in_specs=[pl.BlockSpec((None, rows, HEAD_W), per_seq),
                  pl.BlockSpec((None,) + kn.shape[1:], per_seq),
                  pl.BlockSpec((None,) + vn.shape[1:], per_seq),
                  pl.BlockSpec(bfar.shape, const2),
                  pl.BlockSpec(blast.shape, const2),
                  pl.BlockSpec(bnew.shape, const2),
                  small, small, small, small,
                  pl.BlockSpec((1, DIFF_V_DIM), const2)]
                 + [page_spec(g) for g in range(g_n)] * 2,
        out_specs=pl.BlockSpec((None, rows // 2, HEAD_W), per_seq),
        scratch_shapes=[pltpu.VMEM((rows, 1), F32), pltpu.VMEM((rows, 1), F32),
                        pltpu.VMEM((rows, HEAD_W), F32)],
    )
    return pl.pallas_call(
        functools.partial(_dec_diff_kernel, n_pages_step=g_n, lam_init=lam_init),
        grid_spec=grid_spec,
        out_shape=jax.ShapeDtypeStruct((n_seq, rows // 2, HEAD_W), F32),
        compiler_params=_cparams(2, 48),
        name="dec_diff",
    )(page_table, q, kn, vn, bfar, blast, bnew, *lam_params, sg,
      *([cache_k] * g_n), *([cache_v] * g_n))


def _suffix_by_head(x, n_heads):
    width = x.shape[1]
    lane = lax.broadcasted_iota(jnp.int32, x.shape, 1)
    incl = x
    tot = x
    sh = n_heads
    while sh < width:
        shifted = pltpu.roll(incl, width - sh, axis=1)
        incl = incl + jnp.where(lane < width - sh, shifted, 0.0)
        tot = tot + pltpu.roll(tot, sh, axis=1)
        sh *= 2
    return incl, tot


def _dec_fox_kernel(pt_ref, q_ref, kn_ref, vn_ref, cncol_ref, cnrow_ref, mfar_ref, mnew_ref,
                    *rest, n_pages_step):
    del pt_ref
    g_n = n_pages_step
    kp, vp, lp = rest[:g_n], rest[g_n:2 * g_n], rest[2 * g_n:3 * g_n]
    o_ref, m_ref, l_ref, acc_ref, later_ref = rest[3 * g_n:]
    j = pl.program_id(1)
    last = pl.num_programs(1) - 1
    krows = kp[0].shape[0]
    scale = FOX_HEAD_DIM ** -0.5

    @pl.when(j == 0)
    def _():
        m_ref[...] = jnp.full(m_ref.shape, NEG, F32)
        l_ref[...] = jnp.zeros(l_ref.shape, F32)
        acc_ref[...] = jnp.zeros(acc_ref.shape, F32)
        later_ref[...] = jnp.zeros(later_ref.shape, F32)

    q = q_ref[...].astype(BF16)
    cn_col = cncol_ref[...]
    mfar = mfar_ref[...]

    later = later_ref[...]
    scores = [None] * g_n
    for g in reversed(range(g_n)):
        logf = lp[g][...]
        incl, tot = _suffix_by_head(logf, FOX_HEADS)
        decay = (incl - logf) + later
        later = later + tot
        s = lax.dot_general(q, kp[g][...].astype(BF16), NT_DIMS, preferred_element_type=F32)
        scores[g] = s * scale + (cn_col + decay) + mfar
    later_ref[...] = later

    def pv_pages(p):
        out = jnp.dot(p[:, 0:krows], vp[0][...].astype(BF16), preferred_element_type=F32)
        for g in range(1, g_n):
            out = out + jnp.dot(p[:, g * krows:(g + 1) * krows], vp[g][...].astype(BF16),
                                preferred_element_type=F32)
        return out

    _online_update(jnp.concatenate(scores, axis=1), pv_pages, m_ref, l_ref, acc_ref)

    @pl.when(j == last)
    def _():
        s = lax.dot_general(q, kn_ref[...].astype(BF16), NT_DIMS, preferred_element_type=F32)
        s = s * scale + (cn_col - cnrow_ref[...]) + mnew_ref[...]
        vn = vn_ref[...].astype(BF16)
        _online_update(s, lambda p: jnp.dot(p, vn, preferred_element_type=F32),
                       m_ref, l_ref, acc_ref)
        o_ref[...] = acc_ref[...] / l_ref[...]


def _dec_fox(page_table, q, kn, vn, cn_col, cn_row, mfar, mnew, cache_k, cache_v, cache_lf, layer,
             n_pages_step):
    n_seq, rows, _ = q.shape
    n_pages = page_table.shape[1]
    g_n = n_pages_step
    n_steps = n_pages // g_n
    krows = cache_k.shape[2]

    def page_of(b, j, pt, g):
        return pt[b, (n_steps - 1 - j) * g_n + g]

    def page_spec(g):
        return pl.BlockSpec((None, None, krows, HEAD_W),
                            lambda b, j, pt: (layer, page_of(b, j, pt, g), 0, 0))

    def logf_spec(g):
        return pl.BlockSpec((None, None, 1, krows),
                            lambda b, j, pt: (layer, page_of(b, j, pt, g), 0, 0))

    const2 = lambda b, j, pt: (0, 0)
    per_seq = lambda b, j, pt: (b, 0, 0)
    grid_spec = pltpu.PrefetchScalarGridSpec(
        num_scalar_prefetch=1,
        grid=(n_seq, n_steps),
        in_specs=[pl.BlockSpec((None, rows, HEAD_W), per_seq),
                  pl.BlockSpec((None,) + kn.shape[1:], per_seq),
                  pl.BlockSpec((None,) + vn.shape[1:], per_seq),
                  pl.BlockSpec((None,) + cn_col.shape[1:], per_seq),
                  pl.BlockSpec((None,) + cn_row.shape[1:], per_seq),
                  pl.BlockSpec(mfar.shape, const2),
                  pl.BlockSpec(mnew.shape, const2)]
                 + [page_spec(g) for g in range(g_n)] * 2
                 + [logf_spec(g) for g in range(g_n)],
        out_specs=pl.BlockSpec((None, rows, HEAD_W), per_seq),
        scratch_shapes=[pltpu.VMEM((rows, 1), F32), pltpu.VMEM((rows, 1), F32),
                        pltpu.VMEM((rows, HEAD_W), F32), pltpu.VMEM((1, krows), F32)],
    )
    return pl.pallas_call(
        functools.partial(_dec_fox_kernel, n_pages_step=g_n),
        grid_spec=grid_spec,
        out_shape=jax.ShapeDtypeStruct((n_seq, rows, HEAD_W), F32),
        compiler_params=_cparams(2, 56),
        name="dec_fox",
    )(page_table, q, kn, vn, cn_col, cn_row, mfar, mnew,
      *([cache_k] * g_n), *([cache_v] * g_n), *([cache_lf] * g_n))


def _t5_bucket(rel):
    n = jnp.maximum(rel, 0)
    max_exact = NUM_BUCKETS // 2
    large = max_exact + (jnp.log(jnp.maximum(n, 1).astype(F32) / max_exact)
                         / math.log(MAX_DISTANCE / max_exact) * (NUM_BUCKETS - max_exact)).astype(jnp.int32)
    large = jnp.minimum(large, NUM_BUCKETS - 1)
    return jnp.where(n < max_exact, n, large)


def _t5_bias(rel_bias, rel):
    bucket = _t5_bucket(rel)[None]
    n_heads = rel_bias.shape[1]
    out = jnp.zeros((n_heads,) + rel.shape, F32)
    for b in range(NUM_BUCKETS):
        out = jnp.where(bucket == b, rel_bias[b].astype(F32).reshape((n_heads,) + (1,) * rel.ndim), out)
    return out


def _prompt_bias_tiles(rel_bias, tq):
    assert tq >= MAX_DISTANCE
    i = jnp.arange(tq, dtype=jnp.int32)[:, None]
    j = jnp.arange(tq, dtype=jnp.int32)[None, :]
    far = _t5_bias(rel_bias, jnp.full((1, 1), 2 * tq, jnp.int32))
    diag = jnp.where(i >= j, _t5_bias(rel_bias, i - j) - far, NEG)
    sub = _t5_bias(rel_bias, tq + i - j) - far
    return jnp.stack([diag, sub], axis=1)


def _sample_diff_tables(rel_bias, past_len, t_new, new_pad):
    heads = DIFF_HEADS
    cols = PAGE_SIZE * heads
    h_idx = jnp.arange(heads)[:, None, None]
    tok = jnp.arange(t_new, dtype=jnp.int32)
    c_pos = jnp.arange(cols, dtype=jnp.int32) // heads
    c_head = jnp.arange(cols) % heads

    def as_rows(x):
        x = x.reshape(heads * t_new, x.shape[-1])
        return jnp.concatenate([x, x], axis=0)

    match = h_idx == c_head[None, None, :]
    rel_last = (past_len + tok)[:, None] - (past_len - PAGE_SIZE + c_pos)[None, :]
    blast = as_rows(jnp.where(match, _t5_bias(rel_bias, rel_last), NEG))
    far = _t5_bias(rel_bias, jnp.full((1, 1), 2 * MAX_DISTANCE, jnp.int32))
    bfar = as_rows(jnp.where(match, jnp.broadcast_to(far, (heads, t_new, cols)), NEG))
    n_idx = jnp.arange(new_pad, dtype=jnp.int32)
    n_tok = n_idx // heads
    n_head = n_idx % heads
    valid = ((h_idx == n_head[None, None, :]) & (n_tok[None, None, :] <= tok[None, :, None])
             & (n_idx[None, None, :] < t_new * heads))
    rel_new = jnp.maximum(tok[:, None] - n_tok[None, :], 0)
    bnew = as_rows(jnp.where(valid, _t5_bias(rel_bias, rel_new), NEG))
    return bfar, blast, bnew


def _sample_fox_masks(t_new, new_pad):
    heads = FOX_HEADS
    rows = heads * t_new
    r_head = jnp.arange(rows) // t_new
    r_tok = jnp.arange(rows) % t_new
    c_head = jnp.arange(PAGE_SIZE * heads) % heads
    mfar = jnp.where(r_head[:, None] == c_head[None, :], 0.0, NEG).astype(F32)
    n_tok = jnp.arange(new_pad) // heads
    n_head = jnp.arange(new_pad) % heads
    valid = ((r_head[:, None] == n_head[None, :]) & (n_tok[None, :] <= r_tok[:, None])
             & (jnp.arange(new_pad)[None, :] < t_new * heads))
    mnew = jnp.where(valid, 0.0, NEG).astype(F32)
    return mfar, mnew


PROMPT_TM = 1024
PROMPT_TQ = 512
FFN_TF = 256
DIFF_PAGES_PER_STEP = 16
FOX_PAGES_PER_STEP = 8
NEW_PAD = 128


def _lam_init(layer):
    return 0.8 - 0.6 * math.exp(-0.3 * layer)


PROJ_TN = 512
EVEN_PLAN = ((F32, 0, 3), (F32, 4, 1), (F32, 5, 1), (BF16, 3, 3))
ODD_PLAN = ((F32, 2, 2), (F32, 4, 2), (BF16, 0, 6))


def _prep_weights(w):
    p = dict(w)
    p["w_in_even"] = w["w_in_even"].astype(BF16)
    p["w_out_even"] = w["w_out_even"].astype(BF16)
    w_forget = w["w_in_odd"][..., 3 * FOX_WIDTH:]
    p["w_in_odd"] = w["w_in_odd"][..., :3 * FOX_WIDTH].astype(BF16)
    p["w_forget"] = jnp.pad(w_forget, ((0, 0), (0, 0), (0, HEAD_W - FOX_HEADS))).astype(BF16)
    p["w_out_odd"] = w["w_out_odd"].astype(BF16)
    p["w_gate_ffn"] = w["w_gate_ffn"].astype(BF16)
    p["w_up_ffn"] = w["w_up_ffn"].astype(BF16)
    p["w_down_ffn"] = w["w_down_ffn"].astype(BF16)
    return p


def _run_prompt(x_in, w, bias_tiles):
    n_seq, t, d = x_in.shape
    tm, tq = PROMPT_TM, PROMPT_TQ
    tps = t // tm
    x = x_in.reshape(n_seq * t, d)
    outs = {k: [] for k in ("k_diff", "v_diff", "k_fox", "v_fox", "logf", "conv_a", "conv_ffn")}
    for l in range(DEPTH):
        g_mix = w["norm_mix_g"][l][None]
        if l % 2 == 0:
            e = l // 2
            za, zk, zv, zb = _proj(x, g_mix, w["w_in_even"][e], EVEN_PLAN, tm, PROJ_TN)
            lam_params = [w[n][e][None] for n in ("lam_q1", "lam_k1", "lam_q2", "lam_k2")]
            ob = _diff_attn(zb, bias_tiles, lam_params, w["subln_g"][e][None], n_seq, t, tq,
                            _lam_init(l))
            prev = jnp.zeros((n_seq, CONV_W - 1, A_WIDTH), F32)
            x, st = _mix_even(x, za, ob, w["conv_a_w"][e], prev, w["w_out_even"][e], tm, tps, 1)
            outs["k_diff"].append(zk.reshape(n_seq, t, DIFF_HEADS, HEAD_W))
            outs["v_diff"].append(zv.reshape(n_seq, t, DIFF_HEADS, HEAD_W))
            outs["conv_a"].append(st[tps - 1::tps])
        else:
            o = l // 2
            zk, zv, zb, f = _proj(x, g_mix, w["w_in_odd"][o], ODD_PLAN, tm, PROJ_TN,
                                  w_tail=w["w_forget"][o])
            f_t = f[:, :FOX_HEADS].reshape(n_seq, t, FOX_HEADS)
            f_t = f_t.transpose(0, 2, 1).reshape(n_seq * FOX_HEADS, t)
            b_col = jnp.tile(w["b_forget"][o], n_seq)[:, None]
            logf_t, c_t = _gates_prompt(f_t, b_col)
            c_row = c_t.reshape(n_seq * FOX_HEADS, t // tq, 1, tq)
            oc = _fox_attn(zb, c_row, n_seq, t, tq)
            x = _mix_odd(x, oc, w["w_out_odd"][o], tm)
            outs["k_fox"].append(zk.reshape(n_seq, t, FOX_HEADS, HEAD_W))
            outs["v_fox"].append(zv.reshape(n_seq, t, FOX_HEADS, HEAD_W))
            outs["logf"].append(logf_t.reshape(n_seq, FOX_HEADS, t).transpose(0, 2, 1))
        prev_f = jnp.zeros((n_seq, CONV_W - 1, D_FF), F32)
        x, st = _ffn(x, w["norm_ffn_g"][l][None], w["w_gate_ffn"][l], w["w_up_ffn"][l],
                     w["conv_ffn_w"][l], prev_f, w["w_down_ffn"][l], w["norm_final_g"][None],
                     tm, FFN_TF, tps, 1, l == DEPTH - 1)
        outs["conv_ffn"].append(st[tps - 1::tps])
    return x.reshape(n_seq, t, d), {k: jnp.stack(v) for k, v in outs.items()}


def _time_major_state(s):
    return s.transpose(1, 0, 2).reshape(1, -1, s.shape[-1])


def _state_from_time_major(s, n_seq):
    return s.reshape(CONV_W - 1, n_seq, s.shape[-1]).transpose(1, 0, 2)


def _run_sample(x_in, past, w):
    (cache_k_diff, cache_v_diff, cache_k_fox, cache_v_fox, cache_logf_fox,
     state_conv_a, state_conv_ffn, page_table) = past
    n_seq, t, d = x_in.shape
    m = n_seq * t
    past_len = page_table.shape[1] * PAGE_SIZE
    ck_diff = cache_k_diff.reshape(cache_k_diff.shape[:2] + (PAGE_SIZE * DIFF_HEADS, HEAD_W))
    cv_diff = cache_v_diff.reshape(cache_v_diff.shape[:2] + (PAGE_SIZE * DIFF_HEADS, HEAD_W))
    ck_fox = cache_k_fox.reshape(cache_k_fox.shape[:2] + (PAGE_SIZE * FOX_HEADS, HEAD_W))
    cv_fox = cache_v_fox.reshape(cache_v_fox.shape[:2] + (PAGE_SIZE * FOX_HEADS, HEAD_W))
    clf_fox = cache_logf_fox.reshape(cache_logf_fox.shape[:2] + (1, PAGE_SIZE * FOX_HEADS))
    bfar, blast, bnew = _sample_diff_tables(w["rel_bias"], past_len, t, NEW_PAD)
    mfar, mnew = _sample_fox_masks(t, NEW_PAD)

    def per_seq(a, heads):
        return a.reshape(t, n_seq, heads, HEAD_W).transpose(1, 0, 2, 3)

    def pad_new(a):
        a = a.reshape(n_seq, -1, HEAD_W)
        return jnp.pad(a, ((0, 0), (0, NEW_PAD - a.shape[1]), (0, 0)))

    x = x_in.transpose(1, 0, 2).reshape(m, d)
    outs = {k: [] for k in ("k_diff", "v_diff", "k_fox", "v_fox", "logf", "conv_a", "conv_ffn")}
    for l in range(DEPTH):
        g_mix = w["norm_mix_g"][l][None]
        if l % 2 == 0:
            e = l // 2
            z, zk, zv, zb = _proj(x, g_mix, w["w_in_even"][e], EVEN_PLAN, m, PROJ_TN)
            q = per_seq(zb[:, :DIFF_QK_WIDTH], DIFF_HEADS)
            k = per_seq(zk, DIFF_HEADS)
            v = per_seq(zv, DIFF_HEADS)
            q_rows = q.transpose(0, 2, 1, 3).reshape(n_seq, DIFF_HEADS * t, HEAD_W)
            q_rows = jnp.concatenate([q_rows, q_rows], axis=1)
            lam_params = [w[n][e][None] for n in ("lam_q1", "lam_k1", "lam_q2", "lam_k2")]
            o_dec = _dec_diff(page_table, q_rows, pad_new(k), pad_new(v), bfar, blast, bnew,
                              lam_params, w["subln_g"][e][None], ck_diff, cv_diff, e,
                              DIFF_PAGES_PER_STEP, _lam_init(l))
            ob = o_dec.reshape(n_seq, DIFF_HEADS, t, HEAD_W).transpose(2, 0, 1, 3)
            ob = ob.reshape(m, B_WIDTH).astype(BF16)
            x, st = _mix_even(x, z, ob, w["conv_a_w"][e], _time_major_state(state_conv_a[e]),
                              w["w_out_even"][e], m, 1, n_seq)
            outs["k_diff"].append(k)
            outs["v_diff"].append(v)
            outs["conv_a"].append(_state_from_time_major(st, n_seq))
        else:
            o = l // 2
            zk, zv, zb, f = _proj(x, g_mix, w["w_in_odd"][o], ODD_PLAN, m, PROJ_TN,
                                  w_tail=w["w_forget"][o])
            q = per_seq(zb[:, :FOX_WIDTH], FOX_HEADS)
            k = per_seq(zk, FOX_HEADS)
            v = per_seq(zv, FOX_HEADS)
            logf, cn = _gates_sample(f[:, :FOX_HEADS], w["b_forget"][o][None], n_seq, t)
            cn = cn.reshape(t, n_seq, FOX_HEADS)
            cn_col = cn.transpose(1, 2, 0).reshape(n_seq, FOX_HEADS * t, 1)
            cn_row = cn.transpose(1, 0, 2).reshape(n_seq, 1, t * FOX_HEADS)
            cn_row = jnp.pad(cn_row, ((0, 0), (0, 0), (0, NEW_PAD - t * FOX_HEADS)))
            q_rows = q.transpose(0, 2, 1, 3).reshape(n_seq, FOX_HEADS * t, HEAD_W)
            o_dec = _dec_fox(page_table, q_rows, pad_new(k), pad_new(v), cn_col, cn_row, mfar, mnew,
                             ck_fox, cv_fox, clf_fox, o, FOX_PAGES_PER_STEP)
            oc = o_dec.reshape(n_seq, FOX_HEADS, t, HEAD_W).transpose(2, 0, 1, 3)
            oc = oc.reshape(m, FOX_WIDTH).astype(BF16)
            x = _mix_odd(x, oc, w["w_out_odd"][o], m)
            outs["k_fox"].append(k)
            outs["v_fox"].append(v)
            outs["logf"].append(logf.reshape(t, n_seq, FOX_HEADS).transpose(1, 0, 2))
        x, st = _ffn(x, w["norm_ffn_g"][l][None], w["w_gate_ffn"][l], w["w_up_ffn"][l],
                     w["conv_ffn_w"][l], _time_major_state(state_conv_ffn[l]), w["w_down_ffn"][l],
                     w["norm_final_g"][None], m, FFN_TF, 1, n_seq, l == DEPTH - 1)
        outs["conv_ffn"].append(_state_from_time_major(st, n_seq))
    y = x.reshape(t, n_seq, d).transpose(1, 0, 2)
    return y, {k: jnp.stack(v) for k, v in outs.items()}


def kernel(x_prompt, x_sample, cache_k_diff, cache_v_diff, cache_k_fox, cache_v_fox, cache_logf_fox,
           state_conv_a, state_conv_ffn, page_table,
           norm_mix_g, norm_ffn_g, norm_final_g, w_in_even, w_out_even, conv_a_w,
           lam_q1, lam_k1, lam_q2, lam_k2, subln_g, rel_bias,
           w_in_odd, b_forget, w_out_odd, w_gate_ffn, w_up_ffn, conv_ffn_w, w_down_ffn):
    w = _prep_weights(dict(
        norm_mix_g=norm_mix_g, norm_ffn_g=norm_ffn_g, norm_final_g=norm_final_g,
        w_in_even=w_in_even, w_out_even=w_out_even, conv_a_w=conv_a_w,
        lam_q1=lam_q1, lam_k1=lam_k1, lam_q2=lam_q2, lam_k2=lam_k2, subln_g=subln_g,
        rel_bias=rel_bias, w_in_odd=w_in_odd, b_forget=b_forget, w_out_odd=w_out_odd,
        w_gate_ffn=w_gate_ffn, w_up_ffn=w_up_ffn, conv_ffn_w=conv_ffn_w, w_down_ffn=w_down_ffn))
    bias_tiles = _prompt_bias_tiles(rel_bias, PROMPT_TQ)
    y_p, sp = _run_prompt(x_prompt, w, bias_tiles)
    past = (cache_k_diff, cache_v_diff, cache_k_fox, cache_v_fox, cache_logf_fox,
            state_conv_a, state_conv_ffn, page_table)
    y_s, ss = _run_sample(x_sample, past, w)
    order = ("k_diff", "v_diff", "k_fox", "v_fox", "logf", "conv_a", "conv_ffn")
    return (y_p, y_s) + tuple(sp[k] for k in order) + tuple(ss[k] for k in order)
```

```python
import functools
import math

import jax
import jax.numpy as jnp
from jax import lax
from jax.experimental import pallas as pl
from jax.experimental.pallas import tpu as pltpu

D_MODEL = 1024
DEPTH = 4
PAGE_SIZE = 128
CONV_W = 3
A_WIDTH = D_MODEL // 2
DIFF_HEADS = 4
DIFF_QK_DIM = D_MODEL // 16
DIFF_V_DIM = 2 * DIFF_QK_DIM
HEAD_W = 128
DIFF_QK_WIDTH = DIFF_HEADS * 2 * DIFF_QK_DIM
B_WIDTH = DIFF_HEADS * DIFF_V_DIM
EVEN_IN = 3 * A_WIDTH + 2 * DIFF_QK_WIDTH + B_WIDTH
FOX_HEADS = 8
FOX_HEAD_DIM = D_MODEL // FOX_HEADS
FOX_WIDTH = FOX_HEADS * FOX_HEAD_DIM
NUM_BUCKETS = 32
MAX_DISTANCE = 128
D_FF = (11 * D_MODEL) // 4
EPS = 1e-6
NEG = -1e30
LOG2E = 1.4426950408889634
FOX_HEADS_PER_STEP = 4

F32 = jnp.float32
BF16 = jnp.bfloat16
MIB = 1024 * 1024
NT_DIMS = (((1,), (1,)), ((), ()))


def _cparams(n_axes, vmem_mib):
    return pltpu.CompilerParams(dimension_semantics=("arbitrary",) * n_axes,
                                vmem_limit_bytes=vmem_mib * MIB)


def _rms(x, g):
    ms = jnp.mean(x * x, axis=-1, keepdims=True)
    return x * lax.rsqrt(ms + EPS) * g


def _halo_rows(shift):
    return -(-(2 * shift) // 8) * 8


def _proj_kernel(x_ref, g_ref, w_ref, *rest, plan, tn, has_tail):
    outs = rest[has_tail:]
    tm = x_ref.shape[0]
    h = _rms(x_ref[...], g_ref[...]).astype(BF16)
    if has_tail:
        outs[-1][...] = jnp.dot(h, rest[0][...], preferred_element_type=F32)
    heads_per_tile = tn // HEAD_W
    for c in range(w_ref.shape[1] // tn):
        z = jnp.dot(h, w_ref[:, c * tn:(c + 1) * tn], preferred_element_type=F32)
        for o_ref, (dtype, first, count, by_head) in zip(outs, plan):
            if not first <= c < first + count:
                continue
            if by_head:
                n_heads = count * heads_per_tile
                for hh in range(heads_per_tile):
                    hd = (c - first) * heads_per_tile + hh
                    o_ref[pl.ds(hd, tm, stride=n_heads), :] = (
                        z[:, hh * HEAD_W:(hh + 1) * HEAD_W].astype(dtype))
            else:
                o_ref[:, (c - first) * tn:(c - first + 1) * tn] = z.astype(dtype)


def _proj(x, g, w, plan, tm, tn, w_tail=None):
    m, d = x.shape
    has_tail = w_tail is not None
    in_specs = [pl.BlockSpec((tm, d), lambda i: (i, 0)),
                pl.BlockSpec((1, d), lambda i: (0, 0)),
                pl.BlockSpec(w.shape, lambda i: (0, 0))]
    out_specs, out_shape = [], []
    for dtype, _, count, by_head in plan:
        heads = count * tn // HEAD_W
        rows, cols = (heads, HEAD_W) if by_head else (1, count * tn)
        out_specs.append(pl.BlockSpec((tm * rows, cols), lambda i: (i, 0)))
        out_shape.append(jax.ShapeDtypeStruct((m * rows, cols), dtype))
    args = [x, g, w]
    if has_tail:
        in_specs.append(pl.BlockSpec(w_tail.shape, lambda i: (0, 0)))
        out_specs.append(pl.BlockSpec((tm, w_tail.shape[1]), lambda i: (i, 0)))
        out_shape.append(jax.ShapeDtypeStruct((m, w_tail.shape[1]), F32))
        args.append(w_tail)
    return pl.pallas_call(
        functools.partial(_proj_kernel, plan=tuple(plan), tn=tn, has_tail=has_tail),
        grid=(m // tm,),
        in_specs=in_specs,
        out_specs=out_specs,
        out_shape=out_shape,
        compiler_params=_cparams(1, 48),
        name="proj",
    )(*args)


def _softmax_rows(t, m_ref, l_ref, acc_ref, idx):
    m_old = m_ref[idx]
    m_new = jnp.maximum(m_old, jnp.max(t, axis=-1, keepdims=True))
    alpha = jnp.exp2(m_old - m_new)
    p = jnp.exp2(t - jnp.tile(m_new, (1, t.shape[1] // HEAD_W)))
    m_ref[idx] = m_new
    l_ref[idx] = alpha * l_ref[idx] + jnp.sum(p, axis=-1, keepdims=True)
    acc_ref[idx] = alpha * acc_ref[idx]
    return p.astype(BF16)


def _diff_lambda(lq1_ref, lk1_ref, lq2_ref, lk2_ref, lam_init):
    return (jnp.exp(jnp.sum(lq1_ref[...] * lk1_ref[...], axis=-1, keepdims=True))
            - jnp.exp(jnp.sum(lq2_ref[...] * lk2_ref[...], axis=-1, keepdims=True)) + lam_init)


def _diff_attn_kernel(q_ref, k_ref, v_ref, bias_ref, lq1_ref, lk1_ref, lq2_ref, lk2_ref, sg_ref,
                      o_ref, m_ref, l_ref, acc_ref, *, tq, lam_init):
    qi = pl.program_id(2)
    lane = lax.broadcasted_iota(jnp.int32, (tq, HEAD_W), 1)
    qs = q_ref[...].astype(F32) * (DIFF_QK_DIM ** -0.5)
    q_maps = (jnp.where(lane < DIFF_QK_DIM, qs, 0.0).astype(BF16),
              jnp.where(lane >= DIFF_QK_DIM, qs, 0.0).astype(BF16))
    m_ref[...] = jnp.full(m_ref.shape, NEG, F32)
    l_ref[...] = jnp.zeros(l_ref.shape, F32)
    acc_ref[...] = jnp.zeros(acc_ref.shape, F32)

    def step(kt, kind):
        rows = pl.ds(pl.multiple_of(kt * tq, tq), tq)
        k = k_ref[rows, :]
        v = v_ref[rows, :]
        for mi in range(2):
            t = lax.dot_general(q_maps[mi], k, NT_DIMS, preferred_element_type=F32) * LOG2E
            if kind is not None:
                t = t + bias_ref[kind] * LOG2E
            p = _softmax_rows(t, m_ref, l_ref, acc_ref, mi)
            acc_ref[mi] += jnp.dot(p, v, preferred_element_type=F32)

    def far_body(kt, carry):
        step(kt, None)
        return carry

    lax.fori_loop(0, jnp.maximum(qi - 1, 0), far_body, 0)

    @pl.when(qi >= 1)
    def _():
        step(qi - 1, 1)

    step(qi, 0)

    lam = _diff_lambda(lq1_ref, lk1_ref, lq2_ref, lk2_ref, lam_init)
    o = acc_ref[0] / l_ref[0] - lam * (acc_ref[1] / l_ref[1])
    o_ref[...] = (_rms(o, sg_ref[...]) * (1.0 - lam_init)).astype(BF16)


def _diff_attn(zb, bias, lam_params, sg, n_seq, t, tq, lam_init):
    m = zb.shape[0]
    nq = t // tq
    q_col, k_col, v_col = 0, DIFF_HEADS, 2 * DIFF_HEADS
    small = pl.BlockSpec((1, DIFF_QK_DIM), lambda b, h, i: (0, 0))
    return pl.pallas_call(
        functools.partial(_diff_attn_kernel, tq=tq, lam_init=lam_init),
        grid=(n_seq, DIFF_HEADS, nq),
        in_specs=[pl.BlockSpec((tq, HEAD_W), lambda b, h, i: (b * nq + i, q_col + h)),
                  pl.BlockSpec((t, HEAD_W), lambda b, h, i: (b, k_col + h)),
                  pl.BlockSpec((t, HEAD_W), lambda b, h, i: (b, v_col + h)),
                  pl.BlockSpec((None, 2, tq, tq), lambda b, h, i: (h, 0, 0, 0)),
                  small, small, small, small,
                  pl.BlockSpec((1, DIFF_V_DIM), lambda b, h, i: (0, 0))],
        out_specs=pl.BlockSpec((tq, HEAD_W), lambda b, h, i: (b * nq + i, h)),
        out_shape=jax.ShapeDtypeStruct((m, B_WIDTH), BF16),
        scratch_shapes=[pltpu.VMEM((2, tq, HEAD_W), F32)] * 3,
        compiler_params=_cparams(3, 40),
        name="diff_attn",
    )(zb, zb, zb, bias, *lam_params, sg)


def _fox_attn_kernel(q_ref, k_ref, v_ref, ck_ref, o_ref, m_ref, l_ref, acc_ref, *, tq):
    qi = pl.program_id(2)
    scale2 = (FOX_HEAD_DIM ** -0.5) * LOG2E
    row = lax.broadcasted_iota(jnp.int32, (tq, tq), 0)
    col = lax.broadcasted_iota(jnp.int32, (tq, tq), 1)
    m_ref[...] = jnp.full(m_ref.shape, NEG, F32)
    l_ref[...] = jnp.zeros(l_ref.shape, F32)
    acc_ref[...] = jnp.zeros(acc_ref.shape, F32)
    heads = [slice(hh * HEAD_W, (hh + 1) * HEAD_W) for hh in range(FOX_HEADS_PER_STEP)]
    qs = [q_ref[:, hs] for hs in heads]
    cqs = [jnp.sum(jnp.where(row == col, ck_ref[hh, qi], 0.0), axis=-1, keepdims=True) * LOG2E
           for hh in range(FOX_HEADS_PER_STEP)]

    def step(kt, diagonal):
        rows = pl.ds(pl.multiple_of(kt * tq, tq), tq)
        for hh, hs in enumerate(heads):
            s = lax.dot_general(qs[hh], k_ref[rows, hs], NT_DIMS, preferred_element_type=F32)
            t = s * scale2 + (cqs[hh] - ck_ref[hh, kt] * LOG2E)
            if diagonal:
                t = jnp.where(row >= col, t, NEG)
            p = _softmax_rows(t, m_ref, l_ref, acc_ref, hh)
            acc_ref[hh] += jnp.dot(p, v_ref[rows, hs], preferred_element_type=F32)

    def far_body(kt, carry):
        step(kt, False)
        return carry

    lax.fori_loop(0, qi, far_body, 0)
    step(qi, True)
    for hh, hs in enumerate(heads):
        o_ref[:, hs] = (acc_ref[hh] / l_ref[hh]).astype(BF16)


def _fox_attn(zb, c_row, n_seq, t, tq):
    m = zb.shape[0]
    nq = t // tq
    hps = FOX_HEADS_PER_STEP
    groups = FOX_HEADS // hps
    width = hps * HEAD_W
    return pl.pallas_call(
        functools.partial(_fox_attn_kernel, tq=tq),
        grid=(n_seq, groups, nq),
        in_specs=[pl.BlockSpec((tq, width), lambda b, h, i: (b * nq + i, h)),
                  pl.BlockSpec((t, width), lambda b, h, i: (b, groups + h)),
                  pl.BlockSpec((t, width), lambda b, h, i: (b, 2 * groups + h)),
                  pl.BlockSpec((hps, nq, 1, tq), lambda b, h, i: (b * groups + h, 0, 0, 0))],
        out_specs=pl.BlockSpec((tq, width), lambda b, h, i: (b * nq + i, h)),
        out_shape=jax.ShapeDtypeStruct((m, FOX_WIDTH), BF16),
        scratch_shapes=[pltpu.VMEM((hps, tq, HEAD_W), F32)] * 3,
        compiler_params=_cparams(3, 40),
        name="fox_attn",
    )(zb, zb, zb, c_row)


def _log_sigmoid(x):
    return -(jnp.maximum(-x, 0.0) + jnp.log1p(jnp.exp(-jnp.abs(x))))


def _split3(x):
    hi = x.astype(BF16)
    r1 = x - hi.astype(F32)
    mid = r1.astype(BF16)
    lo = (r1 - mid.astype(F32)).astype(BF16)
    return hi, mid, lo


def _gates_prompt_kernel(f_ref, b_ref, logf_ref, c_ref, *, chunk):
    rows, t = f_ref.shape
    r = lax.broadcasted_iota(jnp.int32, (chunk, chunk), 0)
    cidx = lax.broadcasted_iota(jnp.int32, (chunk, chunk), 1)
    tri = jnp.where(r <= cidx, 1.0, 0.0).astype(BF16)
    carry = jnp.zeros((rows, 1), F32)
    for ci in range(t // chunk):
        sl = slice(ci * chunk, (ci + 1) * chunk)
        logf = _log_sigmoid(f_ref[:, sl] + b_ref[...])
        logf_ref[:, sl] = logf
        y = carry
        for part in _split3(logf):
            y = y + jnp.dot(part, tri, preferred_element_type=F32)
        c_ref[:, sl] = y
        carry = y[:, chunk - 1:chunk]


def _gates_prompt(f_t, b_col):
    rows, t = f_t.shape
    full = pl.BlockSpec((rows, t), lambda i: (0, 0))
    return pl.pallas_call(
        functools.partial(_gates_prompt_kernel, chunk=256),
        grid=(1,),
        in_specs=[full, pl.BlockSpec((rows, 1), lambda i: (0, 0))],
        out_specs=[full, full],
        out_shape=[jax.ShapeDtypeStruct((rows, t), F32)] * 2,
        compiler_params=_cparams(1, 32),
        name="gates_prompt",
    )(f_t, b_col)


def _gates_sample_kernel(f_ref, b_ref, logf_ref, c_ref, *, n_seq, t):
    logf = _log_sigmoid(f_ref[...] + b_ref[...])
    logf_ref[...] = logf
    run = logf[0:n_seq]
    c_ref[0:n_seq, :] = run
    for ti in range(1, t):
        run = run + logf[ti * n_seq:(ti + 1) * n_seq]
        c_ref[ti * n_seq:(ti + 1) * n_seq, :] = run


def _gates_sample(f, b_row, n_seq, t):
    m, h = f.shape
    full = pl.BlockSpec((m, h), lambda i: (0, 0))
    return pl.pallas_call(
        functools.partial(_gates_sample_kernel, n_seq=n_seq, t=t),
        grid=(1,),
        in_specs=[full, pl.BlockSpec((1, h), lambda i: (0, 0))],
        out_specs=[full, full],
        out_shape=[jax.ShapeDtypeStruct((m, h), F32)] * 2,
        compiler_params=_cparams(1, 32),
        name="gates_sample",
    )(f, b_row)


def _conv3(buf_ref, cw_ref, cur, tm, shift, halo):
    y = buf_ref[pl.ds(halo - 2 * shift, tm), :] * cw_ref[0:1, :]
    y = y + buf_ref[pl.ds(halo - shift, tm), :] * cw_ref[1:2, :]
    return y + cur * cw_ref[2:3, :]


def _mix_even_kernel(x_ref, ab_ref, ac_ref, ah_ref, ob_ref, cw_ref, prev_ref, w_ref,
                     xo_ref, st_ref, ubuf, *, tm, tps, shift):
    halo = _halo_rows(shift)
    first = (pl.program_id(0) % tps) == 0

    @pl.when(first)
    def _():
        ubuf[pl.ds(halo - 2 * shift, 2 * shift), :] = prev_ref[...]

    @pl.when(jnp.logical_not(first))
    def _():
        ubuf[pl.ds(halo - 2 * shift, 2 * shift), :] = ubuf[pl.ds(halo + tm - 2 * shift, 2 * shift), :]

    u = ac_ref[...] * ah_ref[...]
    ubuf[pl.ds(halo, tm), :] = u
    ya = ab_ref[...] * _conv3(ubuf, cw_ref, u, tm, shift, halo)
    st_ref[...] = ubuf[pl.ds(halo + tm - 2 * shift, 2 * shift), :]
    y = jnp.dot(ya.astype(BF16), w_ref[0:A_WIDTH, :], preferred_element_type=F32)
    y = y + jnp.dot(ob_ref[...], w_ref[A_WIDTH:, :], preferred_element_type=F32)
    xo_ref[...] = x_ref[...] + y


def _mix_even(x, z, ob, cw, prev, w, tm, tps, shift):
    m, d = x.shape
    return pl.pallas_call(
        functools.partial(_mix_even_kernel, tm=tm, tps=tps, shift=shift),
        grid=(m // tm,),
        in_specs=[pl.BlockSpec((tm, d), lambda i: (i, 0)),
                  pl.BlockSpec((tm, A_WIDTH), lambda i: (i, 0)),
                  pl.BlockSpec((tm, A_WIDTH), lambda i: (i, 1)),
                  pl.BlockSpec((tm, A_WIDTH), lambda i: (i, 2)),
                  pl.BlockSpec((tm, B_WIDTH), lambda i: (i, 0)),
                  pl.BlockSpec((CONV_W, A_WIDTH), lambda i: (0, 0)),
                  pl.BlockSpec((None, 2 * shift, A_WIDTH), lambda i: (i // tps, 0, 0)),
                  pl.BlockSpec((A_WIDTH + B_WIDTH, d), lambda i: (0, 0))],
        out_specs=[pl.BlockSpec((tm, d), lambda i: (i, 0)),
                   pl.BlockSpec((None, 2 * shift, A_WIDTH), lambda i: (i, 0, 0))],
        out_shape=[jax.ShapeDtypeStruct((m, d), F32),
                   jax.ShapeDtypeStruct((m // tm, 2 * shift, A_WIDTH), F32)],
        scratch_shapes=[pltpu.VMEM((_halo_rows(shift) + tm, A_WIDTH), F32)],
        compiler_params=_cparams(1, 48),
        name="mix_even",
    )(x, z, z, z, ob, cw, prev, w)


def _mix_odd_kernel(x_ref, o_ref, w_ref, xo_ref):
    xo_ref[...] = x_ref[...] + jnp.dot(o_ref[...], w_ref[...], preferred_element_type=F32)


def _mix_odd(x, o, w, tm):
    m, d = x.shape
    return pl.pallas_call(
        _mix_odd_kernel,
        grid=(m // tm,),
        in_specs=[pl.BlockSpec((tm, d), lambda i: (i, 0)),
                  pl.BlockSpec((tm, FOX_WIDTH), lambda i: (i, 0)),
                  pl.BlockSpec((FOX_WIDTH, d), lambda i: (0, 0))],
        out_specs=pl.BlockSpec((tm, d), lambda i: (i, 0)),
        out_shape=jax.ShapeDtypeStruct((m, d), F32),
        compiler_params=_cparams(1, 40),
        name="mix_odd",
    )(x, o, w)


def _ffn_kernel(x_ref, g_ref, wg_ref, wu_ref, cw_ref, prev_ref, wd_ref, gf_ref,
                xo_ref, st_ref, h_ref, acc_ref, gbuf, carry_ref, *, tm, tps, shift, final_norm):
    halo = _halo_rows(shift)
    i = pl.program_id(0)
    j = pl.program_id(1)
    first = (i % tps) == 0

    @pl.when(j == 0)
    def _():
        h_ref[...] = _rms(x_ref[...], g_ref[...]).astype(BF16)
        acc_ref[...] = jnp.zeros(acc_ref.shape, F32)

    @pl.when(first)
    def _():
        gbuf[pl.ds(halo - 2 * shift, 2 * shift), :] = prev_ref[...]

    @pl.when(jnp.logical_not(first))
    def _():
        gbuf[pl.ds(halo - 2 * shift, 2 * shift), :] = carry_ref[j]

    h = h_ref[...]
    gate = jnp.dot(h, wg_ref[...], preferred_element_type=F32)
    up = jnp.dot(h, wu_ref[...], preferred_element_type=F32)
    gbuf[pl.ds(halo, tm), :] = gate
    tail = gbuf[pl.ds(halo + tm - 2 * shift, 2 * shift), :]
    carry_ref[j] = tail
    st_ref[...] = tail
    gc = _conv3(gbuf, cw_ref, gate, tm, shift, halo)
    act = (gc * (0.5 * jnp.tanh(0.5 * gc) + 0.5)) * up
    acc_ref[...] += jnp.dot(act.astype(BF16), wd_ref[...], preferred_element_type=F32)

    @pl.when(j == pl.num_programs(1) - 1)
    def _():
        xn = x_ref[...] + acc_ref[...]
        xo_ref[...] = _rms(xn, gf_ref[...]) if final_norm else xn


def _ffn(x, g, wg, wu, cw, prev, wd, gf, tm, tf, tps, shift, final_norm):
    m, d = x.shape
    f = wg.shape[1]
    return pl.pallas_call(
        functools.partial(_ffn_kernel, tm=tm, tps=tps, shift=shift, final_norm=final_norm),
        grid=(m // tm, f // tf),
        in_specs=[pl.BlockSpec((tm, d), lambda i, j: (i, 0)),
                  pl.BlockSpec((1, d), lambda i, j: (0, 0)),
                  pl.BlockSpec((d, tf), lambda i, j: (0, j)),
                  pl.BlockSpec((d, tf), lambda i, j: (0, j)),
                  pl.BlockSpec((CONV_W, tf), lambda i, j: (0, j)),
                  pl.BlockSpec((None, 2 * shift, tf), lambda i, j: (i // tps, 0, j)),
                  pl.BlockSpec((tf, d), lambda i, j: (j, 0)),
                  pl.BlockSpec((1, d), lambda i, j: (0, 0))],
        out_specs=[pl.BlockSpec((tm, d), lambda i, j: (i, 0)),
                   pl.BlockSpec((None, 2 * shift, tf), lambda i, j: (i, 0, j))],
        out_shape=[jax.ShapeDtypeStruct((m, d), F32),
                   jax.ShapeDtypeStruct((m // tm, 2 * shift, f), F32)],
        scratch_shapes=[pltpu.VMEM((tm, d), BF16), pltpu.VMEM((tm, d), F32),
                        pltpu.VMEM((_halo_rows(shift) + tm, tf), F32),
                        pltpu.VMEM((f // tf, 2 * shift, tf), F32)],
        compiler_params=_cparams(2, 56),
        name="ffn",
    )(x, g, wg, wu, cw, prev, wd, gf)


def _online_update(s, pv_fn, m_ref, l_ref, acc_ref):
    m_old = m_ref[...]
    m_new = jnp.maximum(m_old, jnp.max(s, axis=-1, keepdims=True))
    alpha = jnp.exp(m_old - m_new)
    p = jnp.exp(s - m_new)
    l_ref[...] = alpha * l_ref[...] + jnp.sum(p, axis=-1, keepdims=True)
    acc_ref[...] = alpha * acc_ref[...] + pv_fn(p.astype(BF16))
    m_ref[...] = m_new


def _dec_diff_kernel(pt_ref, q_ref, kn_ref, vn_ref, bfar_ref, blast_ref, bnew_ref,
                     lq1_ref, lk1_ref, lq2_ref, lk2_ref, sg_ref, *rest, n_pages_step, lam_init):
    del pt_ref
    g_n = n_pages_step
    kp, vp = rest[:g_n], rest[g_n:2 * g_n]
    o_ref, m_ref, l_ref, acc_ref = rest[2 * g_n:]
    j = pl.program_id(1)
    last = pl.num_programs(1) - 1
    rows = q_ref.shape[0]
    krows = kp[0].shape[0]

    @pl.when(j == 0)
    def _():
        m_ref[...] = jnp.full(m_ref.shape, NEG, F32)
        l_ref[...] = jnp.zeros(l_ref.shape, F32)
        acc_ref[...] = jnp.zeros(acc_ref.shape, F32)

    lane = lax.broadcasted_iota(jnp.int32, (rows, HEAD_W), 1)
    row = lax.broadcasted_iota(jnp.int32, (rows, HEAD_W), 0)
    keep = (lane // DIFF_QK_DIM) == (row // (rows // 2))
    q = jnp.where(keep, q_ref[...].astype(F32) * (DIFF_QK_DIM ** -0.5), 0.0).astype(BF16)

    scores = []
    for g in range(g_n):
        s = lax.dot_general(q, kp[g][...].astype(BF16), NT_DIMS, preferred_element_type=F32)
        if g == g_n - 1:
            bias = jnp.where(j == last, blast_ref[...], bfar_ref[...])
        else:
            bias = bfar_ref[...]
        scores.append(s + bias)

    def pv_pages(p):
        out = jnp.dot(p[:, 0:krows], vp[0][...].astype(BF16), preferred_element_type=F32)
        for g in range(1, g_n):
            out = out + jnp.dot(p[:, g * krows:(g + 1) * krows], vp[g][...].astype(BF16),
                                preferred_element_type=F32)
        return out

    _online_update(jnp.concatenate(scores, axis=1), pv_pages, m_ref, l_ref, acc_ref)

    @pl.when(j == last)
    def _():
        s = lax.dot_general(q, kn_ref[...].astype(BF16), NT_DIMS, preferred_element_type=F32)
        vn = vn_ref[...].astype(BF16)
        _online_update(s + bnew_ref[...], lambda p: jnp.dot(p, vn, preferred_element_type=F32),
                       m_ref, l_ref, acc_ref)
        a = acc_ref[...] / l_ref[...]
        lam = _diff_lambda(lq1_ref, lk1_ref, lq2_ref, lk2_ref, lam_init)
        o = a[0:rows // 2] - lam * a[rows // 2:rows]
        o_ref[...] = _rms(o, sg_ref[...]) * (1.0 - lam_init)


def _dec_diff(page_table, q, kn, vn, bfar, blast, bnew, lam_params, sg, cache_k, cache_v, layer,
              n_pages_step, lam_init):
    n_seq, rows, _ = q.shape
    n_pages = page_table.shape[1]
    g_n = n_pages_step
    krows = cache_k.shape[2]

    def page_spec(g):
        return pl.BlockSpec((None, None, krows, HEAD_W),
                            lambda b, j, pt: (layer, pt[b, j * g_n + g], 0, 0))

    const2 = lambda b, j, pt: (0, 0)
    per_seq = lambda b, j, pt: (b, 0, 0)
    small = pl.BlockSpec((1, DIFF_QK_DIM), const2)
    grid_spec = pltpu.PrefetchScalarGridSpec(
        num_scalar_prefetch=1,
        grid=(n_seq, n_pages // g_n),
        in_specs=[pl.BlockSpec((None, rows, HEAD_W), per_seq),
                  pl.BlockSpec((None,) + kn.shape[1:], per_seq),
                  pl.BlockSpec((None,) + vn.shape[1:], per_seq),
                  pl.BlockSpec(bfar.shape, const2),
                  pl.BlockSpec(blast.shape, const2),
                  pl.BlockSpec(bnew.shape, const2),
                  small, small, small, small,
                  pl.BlockSpec((1, DIFF_V_DIM), const2)]
                 + [page_spec(g) for g in range(g_n)] * 2,
        out_specs=pl.BlockSpec((None, rows // 2, HEAD_W), per_seq),
        scratch_shapes=[pltpu.VMEM((rows, 1), F32), pltpu.VMEM((rows, 1), F32),
                        pltpu.VMEM((rows, HEAD_W), F32)],
    )
    return pl.pallas_call(
        functools.partial(_dec_diff_kernel, n_pages_step=g_n, lam_init=lam_init),
        grid_spec=grid_spec,
        out_shape=jax.ShapeDtypeStruct((n_seq, rows // 2, HEAD_W), F32),
        compiler_params=_cparams(2, 48),
        name="dec_diff",
    )(page_table, q, kn, vn, bfar, blast, bnew, *lam_params, sg,
      *([cache_k] * g_n), *([cache_v] * g_n))


def _suffix_by_head(x, n_heads):
    width = x.shape[1]
    lane = lax.broadcasted_iota(jnp.int32, x.shape, 1)
    incl = x
    tot = x
    sh = n_heads
    while sh < width:
        shifted = pltpu.roll(incl, width - sh, axis=1)
        incl = incl + jnp.where(lane < width - sh, shifted, 0.0)
        tot = tot + pltpu.roll(tot, sh, axis=1)
        sh *= 2
    return incl, tot


def _dec_fox_kernel(pt_ref, q_ref, kn_ref, vn_ref, cncol_ref, cnrow_ref, mfar_ref, mnew_ref,
                    *rest, n_pages_step):
    del pt_ref
    g_n = n_pages_step
    kp, vp, lp = rest[:g_n], rest[g_n:2 * g_n], rest[2 * g_n:3 * g_n]
    o_ref, m_ref, l_ref, acc_ref, later_ref = rest[3 * g_n:]
    j = pl.program_id(1)
    last = pl.num_programs(1) - 1
    krows = kp[0].shape[0]
    scale = FOX_HEAD_DIM ** -0.5

    @pl.when(j == 0)
    def _():
        m_ref[...] = jnp.full(m_ref.shape, NEG, F32)
        l_ref[...] = jnp.zeros(l_ref.shape, F32)
        acc_ref[...] = jnp.zeros(acc_ref.shape, F32)
        later_ref[...] = jnp.zeros(later_ref.shape, F32)

    q = q_ref[...].astype(BF16)
    cn_col = cncol_ref[...]
    mfar = mfar_ref[...]

    later = later_ref[...]
    scores = [None] * g_n
    for g in reversed(range(g_n)):
        logf = lp[g][...]
        incl, tot = _suffix_by_head(logf, FOX_HEADS)
        decay = (incl - logf) + later
        later = later + tot
        s = lax.dot_general(q, kp[g][...].astype(BF16), NT_DIMS, preferred_element_type=F32)
        scores[g] = s * scale + (cn_col + decay) + mfar
    later_ref[...] = later

    def pv_pages(p):
        out = jnp.dot(p[:, 0:krows], vp[0][...].astype(BF16), preferred_element_type=F32)
        for g in range(1, g_n):
            out = out + jnp.dot(p[:, g * krows:(g + 1) * krows], vp[g][...].astype(BF16),
                                preferred_element_type=F32)
        return out

    _online_update(jnp.concatenate(scores, axis=1), pv_pages, m_ref, l_ref, acc_ref)

    @pl.when(j == last)
    def _():
        s = lax.dot_general(q, kn_ref[...].astype(BF16), NT_DIMS, preferred_element_type=F32)
        s = s * scale + (cn_col - cnrow_ref[...]) + mnew_ref[...]
        vn = vn_ref[...].astype(BF16)
        _online_update(s, lambda p: jnp.dot(p, vn, preferred_element_type=F32),
                       m_ref, l_ref, acc_ref)
        o_ref[...] = acc_ref[...] / l_ref[...]


def _dec_fox(page_table, q, kn, vn, cn_col, cn_row, mfar, mnew, cache_k, cache_v, cache_lf, layer,
             n_pages_step):
    n_seq, rows, _ = q.shape
    n_pages = page_table.shape[1]
    g_n = n_pages_step
    n_steps = n_pages // g_n
    krows = cache_k.shape[2]

    def page_of(b, j, pt, g):
        return pt[b, (n_steps - 1 - j) * g_n + g]

    def page_spec(g):
        return pl.BlockSpec((None, None, krows, HEAD_W),
                            lambda b, j, pt: (layer, page_of(b, j, pt, g), 0, 0))

    def logf_spec(g):
        return pl.BlockSpec((None, None, 1, krows),
                            lambda b, j, pt: (layer, page_of(b, j, pt, g), 0, 0))

    const2 = lambda b, j, pt: (0, 0)
    per_seq = lambda b, j, pt: (b, 0, 0)
    grid_spec = pltpu.PrefetchScalarGridSpec(
        num_scalar_prefetch=1,
        grid=(n_seq, n_steps),
        in_specs=[pl.BlockSpec((None, rows, HEAD_W), per_seq),
                  pl.BlockSpec((None,) + kn.shape[1:], per_seq),
                  pl.BlockSpec((None,) + vn.shape[1:], per_seq),
                  pl.BlockSpec((None,) + cn_col.shape[1:], per_seq),
                  pl.BlockSpec((None,) + cn_row.shape[1:], per_seq),
                  pl.BlockSpec(mfar.shape, const2),
                  pl.BlockSpec(mnew.shape, const2)]
                 + [page_spec(g) for g in range(g_n)] * 2
                 + [logf_spec(g) for g in range(g_n)],
        out_specs=pl.BlockSpec((None, rows, HEAD_W), per_seq),
        scratch_shapes=[pltpu.VMEM((rows, 1), F32), pltpu.VMEM((rows, 1), F32),
                        pltpu.VMEM((rows, HEAD_W), F32), pltpu.VMEM((1, krows), F32)],
    )
    return pl.pallas_call(
        functools.partial(_dec_fox_kernel, n_pages_step=g_n),
        grid_spec=grid_spec,
        out_shape=jax.ShapeDtypeStruct((n_seq, rows, HEAD_W), F32),
        compiler_params=_cparams(2, 56),
        name="dec_fox",
    )(page_table, q, kn, vn, cn_col, cn_row, mfar, mnew,
      *([cache_k] * g_n), *([cache_v] * g_n), *([cache_lf] * g_n))


def _t5_bucket(rel):
    n = jnp.maximum(rel, 0)
    max_exact = NUM_BUCKETS // 2
    large = max_exact + (jnp.log(jnp.maximum(n, 1).astype(F32) / max_exact)
                         / math.log(MAX_DISTANCE / max_exact) * (NUM_BUCKETS - max_exact)).astype(jnp.int32)
    large = jnp.minimum(large, NUM_BUCKETS - 1)
    return jnp.where(n < max_exact, n, large)


def _t5_bias(rel_bias, rel):
    bucket = _t5_bucket(rel)[None]
    n_heads = rel_bias.shape[1]
    out = jnp.zeros((n_heads,) + rel.shape, F32)
    for b in range(NUM_BUCKETS):
        out = jnp.where(bucket == b, rel_bias[b].astype(F32).reshape((n_heads,) + (1,) * rel.ndim), out)
    return out


def _prompt_bias_tiles(rel_bias, tq):
    assert tq >= MAX_DISTANCE
    i = jnp.arange(tq, dtype=jnp.int32)[:, None]
    j = jnp.arange(tq, dtype=jnp.int32)[None, :]
    far = _t5_bias(rel_bias, jnp.full((1, 1), 2 * tq, jnp.int32))
    diag = jnp.where(i >= j, _t5_bias(rel_bias, i - j) - far, NEG)
    sub = _t5_bias(rel_bias, tq + i - j) - far
    return jnp.stack([diag, sub], axis=1)


def _sample_diff_tables(rel_bias, past_len, t_new, new_pad):
    heads = DIFF_HEADS
    cols = PAGE_SIZE * heads
    h_idx = jnp.arange(heads)[:, None, None]
    tok = jnp.arange(t_new, dtype=jnp.int32)
    c_pos = jnp.arange(cols, dtype=jnp.int32) // heads
    c_head = jnp.arange(cols) % heads

    def as_rows(x):
        x = x.reshape(heads * t_new, x.shape[-1])
        return jnp.concatenate([x, x], axis=0)

    match = h_idx == c_head[None, None, :]
    rel_last = (past_len + tok)[:, None] - (past_len - PAGE_SIZE + c_pos)[None, :]
    blast = as_rows(jnp.where(match, _t5_bias(rel_bias, rel_last), NEG))
    far = _t5_bias(rel_bias, jnp.full((1, 1), 2 * MAX_DISTANCE, jnp.int32))
    bfar = as_rows(jnp.where(match, jnp.broadcast_to(far, (heads, t_new, cols)), NEG))
    n_idx = jnp.arange(new_pad, dtype=jnp.int32)
    n_tok = n_idx // heads
    n_head = n_idx % heads
    valid = ((h_idx == n_head[None, None, :]) & (n_tok[None, None, :] <= tok[None, :, None])
             & (n_idx[None, None, :] < t_new * heads))
    rel_new = jnp.maximum(tok[:, None] - n_tok[None, :], 0)
    bnew = as_rows(jnp.where(valid, _t5_bias(rel_bias, rel_new), NEG))
    return bfar, blast, bnew


def _sample_fox_masks(t_new, new_pad):
    heads = FOX_HEADS
    rows = heads * t_new
    r_head = jnp.arange(rows) // t_new
    r_tok = jnp.arange(rows) % t_new
    c_head = jnp.arange(PAGE_SIZE * heads) % heads
    mfar = jnp.where(r_head[:, None] == c_head[None, :], 0.0, NEG).astype(F32)
    n_tok = jnp.arange(new_pad) // heads
    n_head = jnp.arange(new_pad) % heads
    valid = ((r_head[:, None] == n_head[None, :]) & (n_tok[None, :] <= r_tok[:, None])
             & (jnp.arange(new_pad)[None, :] < t_new * heads))
    mnew = jnp.where(valid, 0.0, NEG).astype(F32)
    return mfar, mnew


MIX_TM = 1024
PROJ_TM = 512
FFN_TM = 512
FFN_TF = 1408
PROMPT_TQ = 512
DIFF_PAGES_PER_STEP = 16
FOX_PAGES_PER_STEP = 16
NEW_PAD = 128


def _lam_init(layer):
    return 0.8 - 0.6 * math.exp(-0.3 * layer)


PROJ_TN = 512
EVEN_PLAN = ((F32, 0, 3, False), (F32, 4, 1, True), (F32, 5, 1, True), (BF16, 3, 3, False))
ODD_PLAN = ((F32, 2, 2, True), (F32, 4, 2, True), (BF16, 0, 6, False))


def _prep_weights(w):
    p = dict(w)
    p["w_in_even"] = w["w_in_even"].astype(BF16)
    p["w_out_even"] = w["w_out_even"].astype(BF16)
    w_forget = w["w_in_odd"][..., 3 * FOX_WIDTH:]
    p["w_in_odd"] = w["w_in_odd"][..., :3 * FOX_WIDTH].astype(BF16)
    p["w_forget"] = jnp.pad(w_forget, ((0, 0), (0, 0), (0, HEAD_W - FOX_HEADS))).astype(BF16)
    p["w_out_odd"] = w["w_out_odd"].astype(BF16)
    p["w_gate_ffn"] = w["w_gate_ffn"].astype(BF16)
    p["w_up_ffn"] = w["w_up_ffn"].astype(BF16)
    p["w_down_ffn"] = w["w_down_ffn"].astype(BF16)
    return p


def _run_prompt(x_in, w, bias_tiles):
    n_seq, t, d = x_in.shape
    tm, tq = MIX_TM, PROMPT_TQ
    tps = t // tm
    ffn_tps = t // FFN_TM
    x = x_in.reshape(n_seq * t, d)
    outs = {k: [] for k in ("k_diff", "v_diff", "k_fox", "v_fox", "logf", "conv_a", "conv_ffn")}
    for l in range(DEPTH):
        g_mix = w["norm_mix_g"][l][None]
        if l % 2 == 0:
            e = l // 2
            za, zk, zv, zb = _proj(x, g_mix, w["w_in_even"][e], EVEN_PLAN, PROJ_TM, PROJ_TN)
            lam_params = [w[n][e][None] for n in ("lam_q1", "lam_k1", "lam_q2", "lam_k2")]
            ob = _diff_attn(zb, bias_tiles, lam_params, w["subln_g"][e][None], n_seq, t, tq,
                            _lam_init(l))
            prev = jnp.zeros((n_seq, CONV_W - 1, A_WIDTH), F32)
            x, st = _mix_even(x, za, ob, w["conv_a_w"][e], prev, w["w_out_even"][e], tm, tps, 1)
            outs["k_diff"].append(zk.reshape(n_seq, t, DIFF_HEADS, HEAD_W))
            outs["v_diff"].append(zv.reshape(n_seq, t, DIFF_HEADS, HEAD_W))
            outs["conv_a"].append(st[tps - 1::tps])
        else:
            o = l // 2
            zk, zv, zb, f = _proj(x, g_mix, w["w_in_odd"][o], ODD_PLAN, PROJ_TM, PROJ_TN,
                                  w_tail=w["w_forget"][o])
            f_t = f[:, :FOX_HEADS].reshape(n_seq, t, FOX_HEADS)
            f_t = f_t.transpose(0, 2, 1).reshape(n_seq * FOX_HEADS, t)
            b_col = jnp.tile(w["b_forget"][o], n_seq)[:, None]
            logf_t, c_t = _gates_prompt(f_t, b_col)
            c_row = c_t.reshape(n_seq * FOX_HEADS, t // tq, 1, tq)
            oc = _fox_attn(zb, c_row, n_seq, t, tq)
            x = _mix_odd(x, oc, w["w_out_odd"][o], tm)
            outs["k_fox"].append(zk.reshape(n_seq, t, FOX_HEADS, HEAD_W))
            outs["v_fox"].append(zv.reshape(n_seq, t, FOX_HEADS, HEAD_W))
            outs["logf"].append(logf_t.reshape(n_seq, FOX_HEADS, t).transpose(0, 2, 1))
        prev_f = jnp.zeros((n_seq, CONV_W - 1, D_FF), F32)
        x, st = _ffn(x, w["norm_ffn_g"][l][None], w["w_gate_ffn"][l], w["w_up_ffn"][l],
                     w["conv_ffn_w"][l], prev_f, w["w_down_ffn"][l], w["norm_final_g"][None],
                     FFN_TM, FFN_TF, ffn_tps, 1, l == DEPTH - 1)
        outs["conv_ffn"].append(st[ffn_tps - 1::ffn_tps])
    return x.reshape(n_seq, t, d), {k: jnp.stack(v) for k, v in outs.items()}


def _time_major_state(s):
    return s.transpose(1, 0, 2).reshape(1, -1, s.shape[-1])


def _state_from_time_major(s, n_seq):
    return s.reshape(CONV_W - 1, n_seq, s.shape[-1]).transpose(1, 0, 2)


def _run_sample(x_in, past, w):
    (cache_k_diff, cache_v_diff, cache_k_fox, cache_v_fox, cache_logf_fox,
     state_conv_a, state_conv_ffn, page_table) = past
    n_seq, t, d = x_in.shape
    m = n_seq * t
    past_len = page_table.shape[1] * PAGE_SIZE
    ck_diff = cache_k_diff.reshape(cache_k_diff.shape[:2] + (PAGE_SIZE * DIFF_HEADS, HEAD_W))
    cv_diff = cache_v_diff.reshape(cache_v_diff.shape[:2] + (PAGE_SIZE * DIFF_HEADS, HEAD_W))
    ck_fox = cache_k_fox.reshape(cache_k_fox.shape[:2] + (PAGE_SIZE * FOX_HEADS, HEAD_W))
    cv_fox = cache_v_fox.reshape(cache_v_fox.shape[:2] + (PAGE_SIZE * FOX_HEADS, HEAD_W))
    clf_fox = cache_logf_fox.reshape(cache_logf_fox.shape[:2] + (1, PAGE_SIZE * FOX_HEADS))
    bfar, blast, bnew = _sample_diff_tables(w["rel_bias"], past_len, t, NEW_PAD)
    mfar, mnew = _sample_fox_masks(t, NEW_PAD)

    def per_seq(a, heads):
        return a.reshape(t, n_seq, heads, HEAD_W).transpose(1, 0, 2, 3)

    def pad_new(a):
        a = a.reshape(n_seq, -1, HEAD_W)
        return jnp.pad(a, ((0, 0), (0, NEW_PAD - a.shape[1]), (0, 0)))

    x = x_in.transpose(1, 0, 2).reshape(m, d)
    outs = {k: [] for k in ("k_diff", "v_diff", "k_fox", "v_fox", "logf", "conv_a", "conv_ffn")}
    for l in range(DEPTH):
        g_mix = w["norm_mix_g"][l][None]
        if l % 2 == 0:
            e = l // 2
            z, zk, zv, zb = _proj(x, g_mix, w["w_in_even"][e], EVEN_PLAN, m, PROJ_TN)
            q = per_seq(zb[:, :DIFF_QK_WIDTH], DIFF_HEADS)
            k = per_seq(zk, DIFF_HEADS)
            v = per_seq(zv, DIFF_HEADS)
            q_rows = q.transpose(0, 2, 1, 3).reshape(n_seq, DIFF_HEADS * t, HEAD_W)
            q_rows = jnp.concatenate([q_rows, q_rows], axis=1)
            lam_params = [w[n][e][None] for n in ("lam_q1", "lam_k1", "lam_q2", "lam_k2")]
            o_dec = _dec_diff(page_table, q_rows, pad_new(k), pad_new(v), bfar, blast, bnew,
                              lam_params, w["subln_g"][e][None], ck_diff, cv_diff, e,
                              DIFF_PAGES_PER_STEP, _lam_init(l))
            ob = o_dec.reshape(n_seq, DIFF_HEADS, t, HEAD_W).transpose(2, 0, 1, 3)
            ob = ob.reshape(m, B_WIDTH).astype(BF16)
            x, st = _mix_even(x, z, ob, w["conv_a_w"][e], _time_major_state(state_conv_a[e]),
                              w["w_out_even"][e], m, 1, n_seq)
            outs["k_diff"].append(k)
            outs["v_diff"].append(v)
            outs["conv_a"].append(_state_from_time_major(st, n_seq))
        else:
            o = l // 2
            zk, zv, zb, f = _proj(x, g_mix, w["w_in_odd"][o], ODD_PLAN, m, PROJ_TN,
                                  w_tail=w["w_forget"][o])
            q = per_seq(zb[:, :FOX_WIDTH], FOX_HEADS)
            k = per_seq(zk, FOX_HEADS)
            v = per_seq(zv, FOX_HEADS)
            logf, cn = _gates_sample(f[:, :FOX_HEADS], w["b_forget"][o][None], n_seq, t)
            cn = cn.reshape(t, n_seq, FOX_HEADS)
            cn_col = cn.transpose(1, 2, 0).reshape(n_seq, FOX_HEADS * t, 1)
            cn_row = cn.transpose(1, 0, 2).reshape(n_seq, 1, t * FOX_HEADS)
            cn_row = jnp.pad(cn_row, ((0, 0), (0, 0), (0, NEW_PAD - t * FOX_HEADS)))
            q_rows = q.transpose(0, 2, 1, 3).reshape(n_seq, FOX_HEADS * t, HEAD_W)
            o_dec = _dec_fox(page_table, q_rows, pad_new(k), pad_new(v), cn_col, cn_row, mfar, mnew,
                             ck_fox, cv_fox, clf_fox, o, FOX_PAGES_PER_STEP)
            oc = o_dec.reshape(n_seq, FOX_HEADS, t, HEAD_W).transpose(2, 0, 1, 3)
            oc = oc.reshape(m, FOX_WIDTH).astype(BF16)
            x = _mix_odd(x, oc, w["w_out_odd"][o], m)
            outs["k_fox"].append(k)
            outs["v_fox"].append(v)
            outs["logf"].append(logf.reshape(t, n_seq, FOX_HEADS).transpose(1, 0, 2))
        x, st = _ffn(x, w["norm_ffn_g"][l][None], w["w_gate_ffn"][l], w["w_up_ffn"][l],
                     w["conv_ffn_w"][l], _time_major_state(state_conv_ffn[l]), w["w_down_ffn"][l],
                     w["norm_final_g"][None], m, FFN_TF, 1, n_seq, l == DEPTH - 1)
        outs["conv_ffn"].append(_state_from_time_major(st, n_seq))
    y = x.reshape(t, n_seq, d).transpose(1, 0, 2)
    return y, {k: jnp.stack(v) for k, v in outs.items()}


def kernel(x_prompt, x_sample, cache_k_diff, cache_v_diff, cache_k_fox, cache_v_fox, cache_logf_fox,
           state_conv_a, state_conv_ffn, page_table,
           norm_mix_g, norm_ffn_g, norm_final_g, w_in_even, w_out_even, conv_a_w,
           lam_q1, lam_k1, lam_q2, lam_k2, subln_g, rel_bias,
           w_in_odd, b_forget, w_out_odd, w_gate_ffn, w_up_ffn, conv_ffn_w, w_down_ffn):
    w = _prep_weights(dict(
        norm_mix_g=norm_mix_g, norm_ffn_g=norm_ffn_g, norm_final_g=norm_final_g,
        w_in_even=w_in_even, w_out_even=w_out_even, conv_a_w=conv_a_w,
        lam_q1=lam_q1, lam_k1=lam_k1, lam_q2=lam_q2, lam_k2=lam_k2, subln_g=subln_g,
        rel_bias=rel_bias, w_in_odd=w_in_odd, b_forget=b_forget, w_out_odd=w_out_odd,
        w_gate_ffn=w_gate_ffn, w_up_ffn=w_up_ffn, conv_ffn_w=conv_ffn_w, w_down_ffn=w_down_ffn))
    bias_tiles = _prompt_bias_tiles(rel_bias, PROMPT_TQ)
    y_p, sp = _run_prompt(x_prompt, w, bias_tiles)
    past = (cache_k_diff, cache_v_diff, cache_k_fox, cache_v_fox, cache_logf_fox,
            state_conv_a, state_conv_ffn, page_table)
    y_s, ss = _run_sample(x_sample, past, w)
    order = ("k_diff", "v_diff", "k_fox", "v_fox", "logf", "conv_a", "conv_ffn")
    return (y_p, y_s) + tuple(sp[k] for k in order) + tuple(ss[k] for k in order)
```

```python
import functools
import math

import jax
import jax.numpy as jnp
from jax import lax
from jax.experimental import pallas as pl
from jax.experimental.pallas import tpu as pltpu

D_MODEL = 1024
DEPTH = 4
PAGE_SIZE = 128
CONV_W = 3
A_WIDTH = D_MODEL // 2
DIFF_HEADS = 4
DIFF_QK_DIM = D_MODEL // 16
DIFF_V_DIM = 2 * DIFF_QK_DIM
HEAD_W = 128
DIFF_QK_WIDTH = DIFF_HEADS * 2 * DIFF_QK_DIM
B_WIDTH = DIFF_HEADS * DIFF_V_DIM
EVEN_IN = 3 * A_WIDTH + 2 * DIFF_QK_WIDTH + B_WIDTH
FOX_HEADS = 8
FOX_HEAD_DIM = D_MODEL // FOX_HEADS
FOX_WIDTH = FOX_HEADS * FOX_HEAD_DIM
NUM_BUCKETS = 32
MAX_DISTANCE = 128
D_FF = (11 * D_MODEL) // 4
EPS = 1e-6
NEG = -1e30
LOG2E = 1.4426950408889634
FOX_HEADS_PER_STEP = 2
DIFF_HEADS_PER_STEP = 2

F32 = jnp.float32
BF16 = jnp.bfloat16
MIB = 1024 * 1024
NT_DIMS = (((1,), (1,)), ((), ()))


def _cparams(n_axes, vmem_mib):
    return pltpu.CompilerParams(dimension_semantics=("arbitrary",) * n_axes,
                                vmem_limit_bytes=vmem_mib * MIB)


def _rms(x, g):
    ms = jnp.mean(x * x, axis=-1, keepdims=True)
    return x * lax.rsqrt(ms + EPS) * g


def _halo_rows(shift):
    return -(-(2 * shift) // 8) * 8


def _proj_kernel(x_ref, g_ref, w_ref, *rest, plan, tn, has_tail):
    outs = rest[has_tail:]
    tm = x_ref.shape[0]
    h = _rms(x_ref[...], g_ref[...]).astype(BF16)
    if has_tail:
        outs[-1][...] = jnp.dot(h, rest[0][...], preferred_element_type=F32)
    heads_per_tile = tn // HEAD_W
    for c in range(w_ref.shape[1] // tn):
        z = jnp.dot(h, w_ref[:, c * tn:(c + 1) * tn], preferred_element_type=F32)
        for o_ref, (dtype, first, count, by_head) in zip(outs, plan):
            if not first <= c < first + count:
                continue
            if by_head:
                n_heads = count * heads_per_tile
                for hh in range(heads_per_tile):
                    hd = (c - first) * heads_per_tile + hh
                    o_ref[pl.ds(hd, tm, stride=n_heads), :] = (
                        z[:, hh * HEAD_W:(hh + 1) * HEAD_W].astype(dtype))
            else:
                o_ref[:, (c - first) * tn:(c - first + 1) * tn] = z.astype(dtype)


def _proj(x, g, w, plan, tm, tn, w_tail=None):
    m, d = x.shape
    has_tail = w_tail is not None
    in_specs = [pl.BlockSpec((tm, d), lambda i: (i, 0)),
                pl.BlockSpec((1, d), lambda i: (0, 0)),
                pl.BlockSpec(w.shape, lambda i: (0, 0))]
    out_specs, out_shape = [], []
    for dtype, _, count, by_head in plan:
        heads = count * tn // HEAD_W
        rows, cols = (heads, HEAD_W) if by_head else (1, count * tn)
        out_specs.append(pl.BlockSpec((tm * rows, cols), lambda i: (i, 0)))
        out_shape.append(jax.ShapeDtypeStruct((m * rows, cols), dtype))
    args = [x, g, w]
    if has_tail:
        in_specs.append(pl.BlockSpec(w_tail.shape, lambda i: (0, 0)))
        out_specs.append(pl.BlockSpec((tm, w_tail.shape[1]), lambda i: (i, 0)))
        out_shape.append(jax.ShapeDtypeStruct((m, w_tail.shape[1]), F32))
        args.append(w_tail)
    return pl.pallas_call(
        functools.partial(_proj_kernel, plan=tuple(plan), tn=tn, has_tail=has_tail),
        grid=(m // tm,),
        in_specs=in_specs,
        out_specs=out_specs,
        out_shape=out_shape,
        compiler_params=_cparams(1, 48),
        name="proj",
    )(*args)


def _softmax_rows(t, m_ref, l_ref, acc_ref, idx):
    m_old = m_ref[idx]
    m_new = jnp.maximum(m_old, jnp.max(t, axis=-1, keepdims=True))
    alpha = jnp.exp2(m_old - m_new)
    p = jnp.exp2(t - jnp.tile(m_new, (1, t.shape[1] // HEAD_W)))
    m_ref[idx] = m_new
    l_ref[idx] = alpha * l_ref[idx] + jnp.sum(p, axis=-1, keepdims=True)
    acc_ref[idx] = alpha * acc_ref[idx]
    return p.astype(BF16)


def _diff_lambda(lq1_ref, lk1_ref, lq2_ref, lk2_ref, lam_init):
    return (jnp.exp(jnp.sum(lq1_ref[...] * lk1_ref[...], axis=-1, keepdims=True))
            - jnp.exp(jnp.sum(lq2_ref[...] * lk2_ref[...], axis=-1, keepdims=True)) + lam_init)


def _diff_attn_kernel(q_ref, k_ref, v_ref, bias_ref, lq1_ref, lk1_ref, lq2_ref, lk2_ref, sg_ref,
                      o_ref, m_ref, l_ref, acc_ref, *, tq, lam_init):
    qi = pl.program_id(2)
    lane = lax.broadcasted_iota(jnp.int32, (tq, HEAD_W), 1)
    heads = [slice(hh * HEAD_W, (hh + 1) * HEAD_W) for hh in range(DIFF_HEADS_PER_STEP)]
    q_maps = []
    for hs in heads:
        qs = q_ref[:, hs].astype(F32) * (DIFF_QK_DIM ** -0.5)
        q_maps.append((jnp.where(lane < DIFF_QK_DIM, qs, 0.0).astype(BF16),
                       jnp.where(lane >= DIFF_QK_DIM, qs, 0.0).astype(BF16)))
    m_ref[...] = jnp.full(m_ref.shape, NEG, F32)
    l_ref[...] = jnp.zeros(l_ref.shape, F32)
    acc_ref[...] = jnp.zeros(acc_ref.shape, F32)

    def step(kt, kind):
        rows = pl.ds(pl.multiple_of(kt * tq, tq), tq)
        for hh, hs in enumerate(heads):
            k = k_ref[rows, hs]
            v = v_ref[rows, hs]
            for mi in range(2):
                t = lax.dot_general(q_maps[hh][mi], k, NT_DIMS, preferred_element_type=F32) * LOG2E
                if kind is not None:
                    t = t + bias_ref[hh, kind] * LOG2E
                p = _softmax_rows(t, m_ref, l_ref, acc_ref, 2 * hh + mi)
                acc_ref[2 * hh + mi] += jnp.dot(p, v, preferred_element_type=F32)

    def far_body(kt, carry):
        step(kt, None)
        return carry

    lax.fori_loop(0, jnp.maximum(qi - 1, 0), far_body, 0)

    @pl.when(qi >= 1)
    def _():
        step(qi - 1, 1)

    step(qi, 0)

    lam = _diff_lambda(lq1_ref, lk1_ref, lq2_ref, lk2_ref, lam_init)
    for hh, hs in enumerate(heads):
        o = acc_ref[2 * hh] / l_ref[2 * hh] - lam * (acc_ref[2 * hh + 1] / l_ref[2 * hh + 1])
        o_ref[:, hs] = (_rms(o, sg_ref[...]) * (1.0 - lam_init)).astype(BF16)


def _diff_attn(zb, bias, lam_params, sg, n_seq, t, tq, lam_init):
    m = zb.shape[0]
    nq = t // tq
    hps = DIFF_HEADS_PER_STEP
    groups = DIFF_HEADS // hps
    width = hps * HEAD_W
    small = pl.BlockSpec((1, DIFF_QK_DIM), lambda b, h, i: (0, 0))
    return pl.pallas_call(
        functools.partial(_diff_attn_kernel, tq=tq, lam_init=lam_init),
        grid=(n_seq, groups, nq),
        in_specs=[pl.BlockSpec((tq, width), lambda b, h, i: (b * nq + i, h)),
                  pl.BlockSpec((t, width), lambda b, h, i: (b, groups + h)),
                  pl.BlockSpec((t, width), lambda b, h, i: (b, 2 * groups + h)),
                  pl.BlockSpec((hps, 2, tq, tq), lambda b, h, i: (h, 0, 0, 0)),
                  small, small, small, small,
                  pl.BlockSpec((1, DIFF_V_DIM), lambda b, h, i: (0, 0))],
        out_specs=pl.BlockSpec((tq, width), lambda b, h, i: (b * nq + i, h)),
        out_shape=jax.ShapeDtypeStruct((m, B_WIDTH), BF16),
        scratch_shapes=[pltpu.VMEM((2 * hps, tq, HEAD_W), F32)] * 3,
        compiler_params=_cparams(3, 40),
        name="diff_attn",
    )(zb, zb, zb, bias, *lam_params, sg)


def _fox_attn_kernel(q_ref, k_ref, v_ref, ck_ref, o_ref, m_ref, l_ref, acc_ref, *, tq):
    qi = pl.program_id(2)
    scale2 = (FOX_HEAD_DIM ** -0.5) * LOG2E
    row = lax.broadcasted_iota(jnp.int32, (tq, tq), 0)
    col = lax.broadcasted_iota(jnp.int32, (tq, tq), 1)
    m_ref[...] = jnp.full(m_ref.shape, NEG, F32)
    l_ref[...] = jnp.zeros(l_ref.shape, F32)
    acc_ref[...] = jnp.zeros(acc_ref.shape, F32)
    heads = [slice(hh * HEAD_W, (hh + 1) * HEAD_W) for hh in range(FOX_HEADS_PER_STEP)]
    qs = [q_ref[:, hs] for hs in heads]
    cqs = [jnp.sum(jnp.where(row == col, ck_ref[hh, qi], 0.0), axis=-1, keepdims=True) * LOG2E
           for hh in range(FOX_HEADS_PER_STEP)]

    def step(kt, diagonal):
        rows = pl.ds(pl.multiple_of(kt * tq, tq), tq)
        for hh, hs in enumerate(heads):
            s = lax.dot_general(qs[hh], k_ref[rows, hs], NT_DIMS, preferred_element_type=F32)
            t = s * scale2 + (cqs[hh] - ck_ref[hh, kt] * LOG2E)
            if diagonal:
                t = jnp.where(row >= col, t, NEG)
            p = _softmax_rows(t, m_ref, l_ref, acc_ref, hh)
            acc_ref[hh] += jnp.dot(p, v_ref[rows, hs], preferred_element_type=F32)

    def far_body(kt, carry):
        step(kt, False)
        return carry

    lax.fori_loop(0, qi, far_body, 0)
    step(qi, True)
    for hh, hs in enumerate(heads):
        o_ref[:, hs] = (acc_ref[hh] / l_ref[hh]).astype(BF16)


def _fox_attn(zb, c_row, n_seq, t, tq):
    m = zb.shape[0]
    nq = t // tq
    hps = FOX_HEADS_PER_STEP
    groups = FOX_HEADS // hps
    width = hps * HEAD_W
    return pl.pallas_call(
        functools.partial(_fox_attn_kernel, tq=tq),
        grid=(n_seq, groups, nq),
        in_specs=[pl.BlockSpec((tq, width), lambda b, h, i: (b * nq + i, h)),
                  pl.BlockSpec((t, width), lambda b, h, i: (b, groups + h)),
                  pl.BlockSpec((t, width), lambda b, h, i: (b, 2 * groups + h)),
                  pl.BlockSpec((hps, nq, 1, tq), lambda b, h, i: (b * groups + h, 0, 0, 0))],
        out_specs=pl.BlockSpec((tq, width), lambda b, h, i: (b * nq + i, h)),
        out_shape=jax.ShapeDtypeStruct((m, FOX_WIDTH), BF16),
        scratch_shapes=[pltpu.VMEM((hps, tq, HEAD_W), F32)] * 3,
        compiler_params=_cparams(3, 40),
        name="fox_attn",
    )(zb, zb, zb, c_row)


def _log_sigmoid(x):
    return -(jnp.maximum(-x, 0.0) + jnp.log1p(jnp.exp(-jnp.abs(x))))


def _split3(x):
    hi = x.astype(BF16)
    r1 = x - hi.astype(F32)
    mid = r1.astype(BF16)
    lo = (r1 - mid.astype(F32)).astype(BF16)
    return hi, mid, lo


def _gates_prompt_kernel(f_ref, b_ref, logf_ref, c_ref, *, chunk):
    rows, t = f_ref.shape
    r = lax.broadcasted_iota(jnp.int32, (chunk, chunk), 0)
    cidx = lax.broadcasted_iota(jnp.int32, (chunk, chunk), 1)
    tri = jnp.where(r <= cidx, 1.0, 0.0).astype(BF16)
    carry = jnp.zeros((rows, 1), F32)
    for ci in range(t // chunk):
        sl = slice(ci * chunk, (ci + 1) * chunk)
        logf = _log_sigmoid(f_ref[:, sl] + b_ref[...])
        logf_ref[:, sl] = logf
        y = carry
        for part in _split3(logf):
            y = y + jnp.dot(part, tri, preferred_element_type=F32)
        c_ref[:, sl] = y
        carry = y[:, chunk - 1:chunk]


def _gates_prompt(f_t, b_col):
    rows, t = f_t.shape
    full = pl.BlockSpec((rows, t), lambda i: (0, 0))
    return pl.pallas_call(
        functools.partial(_gates_prompt_kernel, chunk=256),
        grid=(1,),
        in_specs=[full, pl.BlockSpec((rows, 1), lambda i: (0, 0))],
        out_specs=[full, full],
        out_shape=[jax.ShapeDtypeStruct((rows, t), F32)] * 2,
        compiler_params=_cparams(1, 32),
        name="gates_prompt",
    )(f_t, b_col)


def _gates_sample_kernel(f_ref, b_ref, logf_ref, c_ref, *, n_seq, t):
    logf = _log_sigmoid(f_ref[...] + b_ref[...])
    logf_ref[...] = logf
    run = logf[0:n_seq]
    c_ref[0:n_seq, :] = run
    for ti in range(1, t):
        run = run + logf[ti * n_seq:(ti + 1) * n_seq]
        c_ref[ti * n_seq:(ti + 1) * n_seq, :] = run


def _gates_sample(f, b_row, n_seq, t):
    m, h = f.shape
    full = pl.BlockSpec((m, h), lambda i: (0, 0))
    return pl.pallas_call(
        functools.partial(_gates_sample_kernel, n_seq=n_seq, t=t),
        grid=(1,),
        in_specs=[full, pl.BlockSpec((1, h), lambda i: (0, 0))],
        out_specs=[full, full],
        out_shape=[jax.ShapeDtypeStruct((m, h), F32)] * 2,
        compiler_params=_cparams(1, 32),
        name="gates_sample",
    )(f, b_row)


def _conv3(buf_ref, cw_ref, cur, tm, shift, halo):
    y = buf_ref[pl.ds(halo - 2 * shift, tm), :] * cw_ref[0:1, :]
    y = y + buf_ref[pl.ds(halo - shift, tm), :] * cw_ref[1:2, :]
    return y + cur * cw_ref[2:3, :]


def _mix_even_kernel(x_ref, ab_ref, ac_ref, ah_ref, ob_ref, cw_ref, prev_ref, w_ref,
                     xo_ref, st_ref, ubuf, *, tm, tps, shift):
    halo = _halo_rows(shift)
    first = (pl.program_id(0) % tps) == 0

    @pl.when(first)
    def _():
        ubuf[pl.ds(halo - 2 * shift, 2 * shift), :] = prev_ref[...]

    @pl.when(jnp.logical_not(first))
    def _():
        ubuf[pl.ds(halo - 2 * shift, 2 * shift), :] = ubuf[pl.ds(halo + tm - 2 * shift, 2 * shift), :]

    u = ac_ref[...] * ah_ref[...]
    ubuf[pl.ds(halo, tm), :] = u
    ya = ab_ref[...] * _conv3(ubuf, cw_ref, u, tm, shift, halo)
    st_ref[...] = ubuf[pl.ds(halo + tm - 2 * shift, 2 * shift), :]
    y = jnp.dot(ya.astype(BF16), w_ref[0:A_WIDTH, :], preferred_element_type=F32)
    y = y + jnp.dot(ob_ref[...], w_ref[A_WIDTH:, :], preferred_element_type=F32)
    xo_ref[...] = x_ref[...] + y


def _mix_even(x, z, ob, cw, prev, w, tm, tps, shift):
    m, d = x.shape
    return pl.pallas_call(
        functools.partial(_mix_even_kernel, tm=tm, tps=tps, shift=shift),
        grid=(m // tm,),
        in_specs=[pl.BlockSpec((tm, d), lambda i: (i, 0)),
                  pl.BlockSpec((tm, A_WIDTH), lambda i: (i, 0)),
                  pl.BlockSpec((tm, A_WIDTH), lambda i: (i, 1)),
                  pl.BlockSpec((tm, A_WIDTH), lambda i: (i, 2)),
                  pl.BlockSpec((tm, B_WIDTH), lambda i: (i, 0)),
                  pl.BlockSpec((CONV_W, A_WIDTH), lambda i: (0, 0)),
                  pl.BlockSpec((None, 2 * shift, A_WIDTH), lambda i: (i // tps, 0, 0)),
                  pl.BlockSpec((A_WIDTH + B_WIDTH, d), lambda i: (0, 0))],
        out_specs=[pl.BlockSpec((tm, d), lambda i: (i, 0)),
                   pl.BlockSpec((None, 2 * shift, A_WIDTH), lambda i: (i, 0, 0))],
        out_shape=[jax.ShapeDtypeStruct((m, d), F32),
                   jax.ShapeDtypeStruct((m // tm, 2 * shift, A_WIDTH), F32)],
        scratch_shapes=[pltpu.VMEM((_halo_rows(shift) + tm, A_WIDTH), F32)],
        compiler_params=_cparams(1, 48),
        name="mix_even",
    )(x, z, z, z, ob, cw, prev, w)


def _mix_odd_kernel(x_ref, o_ref, w_ref, xo_ref):
    xo_ref[...] = x_ref[...] + jnp.dot(o_ref[...], w_ref[...], preferred_element_type=F32)


def _mix_odd(x, o, w, tm):
    m, d = x.shape
    return pl.pallas_call(
        _mix_odd_kernel,
        grid=(m // tm,),
        in_specs=[pl.BlockSpec((tm, d), lambda i: (i, 0)),
                  pl.BlockSpec((tm, FOX_WIDTH), lambda i: (i, 0)),
                  pl.BlockSpec((FOX_WIDTH, d), lambda i: (0, 0))],
        out_specs=pl.BlockSpec((tm, d), lambda i: (i, 0)),
        out_shape=jax.ShapeDtypeStruct((m, d), F32),
        compiler_params=_cparams(1, 40),
        name="mix_odd",
    )(x, o, w)


def _ffn_kernel(x_ref, g_ref, wg_ref, wu_ref, cw_ref, prev_ref, wd_ref, gf_ref,
                xo_ref, st_ref, h_ref, acc_ref, gbuf, carry_ref, *, tm, tps, shift, final_norm):
    halo = _halo_rows(shift)
    i = pl.program_id(0)
    j = pl.program_id(1)
    first = (i % tps) == 0

    @pl.when(j == 0)
    def _():
        h_ref[...] = _rms(x_ref[...], g_ref[...]).astype(BF16)
        acc_ref[...] = jnp.zeros(acc_ref.shape, F32)

    @pl.when(first)
    def _():
        gbuf[pl.ds(halo - 2 * shift, 2 * shift), :] = prev_ref[...]

    @pl.when(jnp.logical_not(first))
    def _():
        gbuf[pl.ds(halo - 2 * shift, 2 * shift), :] = carry_ref[j]

    h = h_ref[...]
    gate = jnp.dot(h, wg_ref[...], preferred_element_type=F32)
    up = jnp.dot(h, wu_ref[...], preferred_element_type=F32)
    gbuf[pl.ds(halo, tm), :] = gate
    tail = gbuf[pl.ds(halo + tm - 2 * shift, 2 * shift), :]
    carry_ref[j] = tail
    st_ref[...] = tail
    gc = _conv3(gbuf, cw_ref, gate, tm, shift, halo)
    act = (gc * (0.5 * jnp.tanh(0.5 * gc) + 0.5)) * up
    acc_ref[...] += jnp.dot(act.astype(BF16), wd_ref[...], preferred_element_type=F32)

    @pl.when(j == pl.num_programs(1) - 1)
    def _():
        xn = x_ref[...] + acc_ref[...]
        xo_ref[...] = _rms(xn, gf_ref[...]) if final_norm else xn


def _ffn(x, g, wg, wu, cw, prev, wd, gf, tm, tf, tps, shift, final_norm):
    m, d = x.shape
    f = wg.shape[1]
    return pl.pallas_call(
        functools.partial(_ffn_kernel, tm=tm, tps=tps, shift=shift, final_norm=final_norm),
        grid=(m // tm, f // tf),
        in_specs=[pl.BlockSpec((tm, d), lambda i, j: (i, 0)),
                  pl.BlockSpec((1, d), lambda i, j: (0, 0)),
                  pl.BlockSpec((d, tf), lambda i, j: (0, j)),
                  pl.BlockSpec((d, tf), lambda i, j: (0, j)),
                  pl.BlockSpec((CONV_W, tf), lambda i, j: (0, j)),
                  pl.BlockSpec((None, 2 * shift, tf), lambda i, j: (i // tps, 0, j)),
                  pl.BlockSpec((tf, d), lambda i, j: (j, 0)),
                  pl.BlockSpec((1, d), lambda i, j: (0, 0))],
        out_specs=[pl.BlockSpec((tm, d), lambda i, j: (i, 0)),
                   pl.BlockSpec((None, 2 * shift, tf), lambda i, j: (i, 0, j))],
        out_shape=[jax.ShapeDtypeStruct((m, d), F32),
                   jax.ShapeDtypeStruct((m // tm, 2 * shift, f), F32)],
        scratch_shapes=[pltpu.VMEM((tm, d), BF16), pltpu.VMEM((tm, d), F32),
                        pltpu.VMEM((_halo_rows(shift) + tm, tf), F32),
                        pltpu.VMEM((f // tf, 2 * shift, tf), F32)],
        compiler_params=_cparams(2, 56),
        name="ffn",
    )(x, g, wg, wu, cw, prev, wd, gf)


def _online_update(s, pv_fn, m_ref, l_ref, acc_ref):
    m_old = m_ref[...]
    m_new = jnp.maximum(m_old, jnp.max(s, axis=-1, keepdims=True))
    alpha = jnp.exp(m_old - m_new)
    p = jnp.exp(s - m_new)
    l_ref[...] = alpha * l_ref[...] + jnp.sum(p, axis=-1, keepdims=True)
    acc_ref[...] = alpha * acc_ref[...] + pv_fn(p.astype(BF16))
    m_ref[...] = m_new


def _dec_diff_kernel(pt_ref, q_ref, kn_ref, vn_ref, bfar_ref, blast_ref, bnew_ref,
                     lq1_ref, lk1_ref, lq2_ref, lk2_ref, sg_ref, *rest, n_pages_step, lam_init):
    del pt_ref
    g_n = n_pages_step
    kp, vp = rest[:g_n], rest[g_n:2 * g_n]
    o_ref, m_ref, l_ref, acc_ref = rest[2 * g_n:]
    j = pl.program_id(1)
    last = pl.num_programs(1) - 1
    rows = q_ref.shape[0]
    krows = kp[0].shape[0]

    @pl.when(j == 0)
    def _():
        m_ref[...] = jnp.full(m_ref.shape, NEG, F32)
        l_ref[...] = jnp.zeros(l_ref.shape, F32)
        acc_ref[...] = jnp.zeros(acc_ref.shape, F32)

    lane = lax.broadcasted_iota(jnp.int32, (rows, HEAD_W), 1)
    row = lax.broadcasted_iota(jnp.int32, (rows, HEAD_W), 0)
    keep = (lane // DIFF_QK_DIM) == (row // (rows // 2))
    q = jnp.where(keep, q_ref[...].astype(F32) * (DIFF_QK_DIM ** -0.5), 0.0).astype(BF16)

    scores = []
    for g in range(g_n):
        s = lax.dot_general(q, kp[g][...].astype(BF16), NT_DIMS, preferred_element_type=F32)
        if g == g_n - 1:
            bias = jnp.where(j == last, blast_ref[...], bfar_ref[...])
        else:
            bias = bfar_ref[...]
        scores.append(s + bias)

    def pv_pages(p):
        out = jnp.dot(p[:, 0:krows], vp[0][...].astype(BF16), preferred_element_type=F32)
        for g in range(1, g_n):
            out = out + jnp.dot(p[:, g * krows:(g + 1) * krows], vp[g][...].astype(BF16),
                                preferred_element_type=F32)
        return out

    _online_update(jnp.concatenate(scores, axis=1), pv_pages, m_ref, l_ref, acc_ref)

    @pl.when(j == last)
    def _():
        s = lax.dot_general(q, kn_ref[...].astype(BF16), NT_DIMS, preferred_element_type=F32)
        vn = vn_ref[...].astype(BF16)
        _online_update(s + bnew_ref[...], lambda p: jnp.dot(p, vn, preferred_element_type=F32),
                       m_ref, l_ref, acc_ref)
        a = acc_ref[...] / l_ref[...]
        lam = _diff_lambda(lq1_ref, lk1_ref, lq2_ref, lk2_ref, lam_init)
        o = a[0:rows // 2] - lam * a[rows // 2:rows]
        o_ref[...] = _rms(o, sg_ref[...]) * (1.0 - lam_init)


def _dec_diff(page_table, q, kn, vn, bfar, blast, bnew, lam_params, sg, cache_k, cache_v, layer,
              n_pages_step, lam_init):
    n_seq, rows, _ = q.shape
    n_pages = page_table.shape[1]
    g_n = n_pages_step
    krows = cache_k.shape[2]

    def page_spec(g):
        return pl.BlockSpec((None, None, krows, HEAD_W),
                            lambda b, j, pt: (layer, pt[b, j * g_n + g], 0, 0))

    const2 = lambda b, j, pt: (0, 0)
    per_seq = lambda b, j, pt: (b, 0, 0)
    small = pl.BlockSpec((1, DIFF_QK_DIM), const2)
    grid_spec = pltpu.PrefetchScalarGridSpec(
        num_scalar_prefetch=1,
        grid=(n_seq, n_pages // g_n),
        in_specs=[pl.BlockSpec((None, rows, HEAD_W), per_seq),
                  pl.BlockSpec((None,) + kn.shape[1:], per_seq),
                  pl.BlockSpec((None,) + vn.shape[1:], per_seq),
                  pl.BlockSpec(bfar.shape, const2),
                  pl.BlockSpec(blast.shape, const2),
                  pl.BlockSpec(bnew.shape, const2),
                  small, small, small, small,
                  pl.BlockSpec((1, DIFF_V_DIM), const2)]
                 + [page_spec(g) for g in range(g_n)] * 2,
        out_specs=pl.BlockSpec((None, rows // 2, HEAD_W), per_seq),
        scratch_shapes=[pltpu.VMEM((rows, 1), F32), pltpu.VMEM((rows, 1), F32),
                        pltpu.VMEM((rows, HEAD_W), F32)],
    )
    return pl.pallas_call(
        functools.partial(_dec_diff_kernel, n_pages_step=g_n, lam_init=lam_init),
        grid_spec=grid_spec,
        out_shape=jax.ShapeDtypeStruct((n_seq, rows // 2, HEAD_W), F32),
        compiler_params=_cparams(2, 48),
        name="dec_diff",
    )(page_table, q, kn, vn, bfar, blast, bnew, *lam_params, sg,
      *([cache_k] * g_n), *([cache_v] * g_n))


def _suffix_by_head(x, n_heads):
    width = x.shape[1]
    lane = lax.broadcasted_iota(jnp.int32, x.shape, 1)
    incl = x
    tot = x
    sh = n_heads
    while sh < width:
        shifted = pltpu.roll(incl, width - sh, axis=1)
        incl = incl + jnp.where(lane < width - sh, shifted, 0.0)
        tot = tot + pltpu.roll(tot, sh, axis=1)
        sh *= 2
    return incl, tot


def _dec_fox_kernel(pt_ref, q_ref, kn_ref, vn_ref, cncol_ref, cnrow_ref, mfar_ref, mnew_ref,
                    *rest, n_pages_step):
    del pt_ref
    g_n = n_pages_step
    kp, vp, lp = rest[:g_n], rest[g_n:2 * g_n], rest[2 * g_n:3 * g_n]
    o_ref, m_ref, l_ref, acc_ref, later_ref = rest[3 * g_n:]
    j = pl.program_id(1)
    last = pl.num_programs(1) - 1
    krows = kp[0].shape[0]
    scale = FOX_HEAD_DIM ** -0.5

    @pl.when(j == 0)
    def _():
        m_ref[...] = jnp.full(m_ref.shape, NEG, F32)
        l_ref[...] = jnp.zeros(l_ref.shape, F32)
        acc_ref[...] = jnp.zeros(acc_ref.shape, F32)
        later_ref[...] = jnp.zeros(later_ref.shape, F32)

    q = q_ref[...].astype(BF16)
    cn_col = cncol_ref[...]
    mfar = mfar_ref[...]

    later = later_ref[...]
    scores = [None] * g_n
    for g in reversed(range(g_n)):
        logf = lp[g][...]
        incl, tot = _suffix_by_head(logf, FOX_HEADS)
        decay = (incl - logf) + later
        later = later + tot
        s = lax.dot_general(q, kp[g][...].astype(BF16), NT_DIMS, preferred_element_type=F32)
        scores[g] = s * scale + (cn_col + decay) + mfar
    later_ref[...] = later

    def pv_pages(p):
        out = jnp.dot(p[:, 0:krows], vp[0][...].astype(BF16), preferred_element_type=F32)
        for g in range(1, g_n):
            out = out + jnp.dot(p[:, g * krows:(g + 1) * krows], vp[g][...].astype(BF16),
                                preferred_element_type=F32)
        return out

    _online_update(jnp.concatenate(scores, axis=1), pv_pages, m_ref, l_ref, acc_ref)

    @pl.when(j == last)
    def _():
        s = lax.dot_general(q, kn_ref[...].astype(BF16), NT_DIMS, preferred_element_type=F32)
        s = s * scale + (cn_col - cnrow_ref[...]) + mnew_ref[...]
        vn = vn_ref[...].astype(BF16)
        _online_update(s, lambda p: jnp.dot(p, vn, preferred_element_type=F32),
                       m_ref, l_ref, acc_ref)
        o_ref[...] = acc_ref[...] / l_ref[...]


def _dec_fox(page_table, q, kn, vn, cn_col, cn_row, mfar, mnew, cache_k, cache_v, cache_lf, layer,
             n_pages_step):
    n_seq, rows, _ = q.shape
    n_pages = page_table.shape[1]
    g_n = n_pages_step
    n_steps = n_pages // g_n
    krows = cache_k.shape[2]

    def page_of(b, j, pt, g):
        return pt[b, (n_steps - 1 - j) * g_n + g]

    def page_spec(g):
        return pl.BlockSpec((None, None, krows, HEAD_W),
                            lambda b, j, pt: (layer, page_of(b, j, pt, g), 0, 0))

    def logf_spec(g):
        return pl.BlockSpec((None, None, 1, krows),
                            lambda b, j, pt: (layer, page_of(b, j, pt, g), 0, 0))

    const2 = lambda b, j, pt: (0, 0)
    per_seq = lambda b, j, pt: (b, 0, 0)
    grid_spec = pltpu.PrefetchScalarGridSpec(
        num_scalar_prefetch=1,
        grid=(n_seq, n_steps),
        in_specs=[pl.BlockSpec((None, rows, HEAD_W), per_seq),
                  pl.BlockSpec((None,) + kn.shape[1:], per_seq),
                  pl.BlockSpec((None,) + vn.shape[1:], per_seq),
                  pl.BlockSpec((None,) + cn_col.shape[1:], per_seq),
                  pl.BlockSpec((None,) + cn_row.shape[1:], per_seq),
                  pl.BlockSpec(mfar.shape, const2),
                  pl.BlockSpec(mnew.shape, const2)]
                 + [page_spec(g) for g in range(g_n)] * 2
                 + [logf_spec(g) for g in range(g_n)],
        out_specs=pl.BlockSpec((None, rows, HEAD_W), per_seq),
        scratch_shapes=[pltpu.VMEM((rows, 1), F32), pltpu.VMEM((rows, 1), F32),
                        pltpu.VMEM((rows, HEAD_W), F32), pltpu.VMEM((1, krows), F32)],
    )
    return pl.pallas_call(
        functools.partial(_dec_fox_kernel, n_pages_step=g_n),
        grid_spec=grid_spec,
        out_shape=jax.ShapeDtypeStruct((n_seq, rows, HEAD_W), F32),
        compiler_params=_cparams(2, 56),
        name="dec_fox",
    )(page_table, q, kn, vn, cn_col, cn_row, mfar, mnew,
      *([cache_k] * g_n), *([cache_v] * g_n), *([cache_lf] * g_n))


def _t5_bucket(rel):
    n = jnp.maximum(rel, 0)
    max_exact = NUM_BUCKETS // 2
    large = max_exact + (jnp.log(jnp.maximum(n, 1).astype(F32) / max_exact)
                         / math.log(MAX_DISTANCE / max_exact) * (NUM_BUCKETS - max_exact)).astype(jnp.int32)
    large = jnp.minimum(large, NUM_BUCKETS - 1)
    return jnp.where(n < max_exact, n, large)


def _t5_bias(rel_bias, rel):
    bucket = _t5_bucket(rel)[None]
    n_heads = rel_bias.shape[1]
    out = jnp.zeros((n_heads,) + rel.shape, F32)
    for b in range(NUM_BUCKETS):
        out = jnp.where(bucket == b, rel_bias[b].astype(F32).reshape((n_heads,) + (1,) * rel.ndim), out)
    return out


def _prompt_bias_tiles(rel_bias, tq):
    assert tq >= MAX_DISTANCE
    i = jnp.arange(tq, dtype=jnp.int32)[:, None]
    j = jnp.arange(tq, dtype=jnp.int32)[None, :]
    far = _t5_bias(rel_bias, jnp.full((1, 1), 2 * tq, jnp.int32))
    diag = jnp.where(i >= j, _t5_bias(rel_bias, i - j) - far, NEG)
    sub = _t5_bias(rel_bias, tq + i - j) - far
    return jnp.stack([diag, sub], axis=1)


def _sample_diff_tables(rel_bias, past_len, t_new, new_pad):
    heads = DIFF_HEADS
    cols = PAGE_SIZE * heads
    h_idx = jnp.arange(heads)[:, None, None]
    tok = jnp.arange(t_new, dtype=jnp.int32)
    c_pos = jnp.arange(cols, dtype=jnp.int32) // heads
    c_head = jnp.arange(cols) % heads

    def as_rows(x):
        x = x.reshape(heads * t_new, x.shape[-1])
        return jnp.concatenate([x, x], axis=0)

    match = h_idx == c_head[None, None, :]
    rel_last = (past_len + tok)[:, None] - (past_len - PAGE_SIZE + c_pos)[None, :]
    blast = as_rows(jnp.where(match, _t5_bias(rel_bias, rel_last), NEG))
    far = _t5_bias(rel_bias, jnp.full((1, 1), 2 * MAX_DISTANCE, jnp.int32))
    bfar = as_rows(jnp.where(match, jnp.broadcast_to(far, (heads, t_new, cols)), NEG))
    n_idx = jnp.arange(new_pad, dtype=jnp.int32)
    n_tok = n_idx // heads
    n_head = n_idx % heads
    valid = ((h_idx == n_head[None, None, :]) & (n_tok[None, None, :] <= tok[None, :, None])
             & (n_idx[None, None, :] < t_new * heads))
    rel_new = jnp.maximum(tok[:, None] - n_tok[None, :], 0)
    bnew = as_rows(jnp.where(valid, _t5_bias(rel_bias, rel_new), NEG))
    return bfar, blast, bnew


def _sample_fox_masks(t_new, new_pad):
    heads = FOX_HEADS
    rows = heads * t_new
    r_head = jnp.arange(rows) // t_new
    r_tok = jnp.arange(rows) % t_new
    c_head = jnp.arange(PAGE_SIZE * heads) % heads
    mfar = jnp.where(r_head[:, None] == c_head[None, :], 0.0, NEG).astype(F32)
    n_tok = jnp.arange(new_pad) // heads
    n_head = jnp.arange(new_pad) % heads
    valid = ((r_head[:, None] == n_head[None, :]) & (n_tok[None, :] <= r_tok[:, None])
             & (jnp.arange(new_pad)[None, :] < t_new * heads))
    mnew = jnp.where(valid, 0.0, NEG).astype(F32)
    return mfar, mnew


MIX_TM = 1024
PROJ_TM = 512
FFN_TM = 512
FFN_TF = 1408
DIFF_TQ = 512
FOX_TQ = 1024
DIFF_PAGES_PER_STEP = 16
FOX_PAGES_PER_STEP = 16
NEW_PAD = 128


def _lam_init(layer):
    return 0.8 - 0.6 * math.exp(-0.3 * layer)


PROJ_TN = 512
EVEN_PLAN = ((F32, 0, 3, False), (F32, 4, 1, True), (F32, 5, 1, True), (BF16, 3, 3, False))
ODD_PLAN = ((F32, 2, 2, True), (F32, 4, 2, True), (BF16, 0, 6, False))


def _prep_weights(w):
    p = dict(w)
    p["w_in_even"] = w["w_in_even"].astype(BF16)
    p["w_out_even"] = w["w_out_even"].astype(BF16)
    w_forget = w["w_in_odd"][..., 3 * FOX_WIDTH:]
    p["w_in_odd"] = w["w_in_odd"][..., :3 * FOX_WIDTH].astype(BF16)
    p["w_forget"] = jnp.pad(w_forget, ((0, 0), (0, 0), (0, HEAD_W - FOX_HEADS))).astype(BF16)
    p["w_out_odd"] = w["w_out_odd"].astype(BF16)
    p["w_gate_ffn"] = w["w_gate_ffn"].astype(BF16)
    p["w_up_ffn"] = w["w_up_ffn"].astype(BF16)
    p["w_down_ffn"] = w["w_down_ffn"].astype(BF16)
    return p


def _run_prompt(x_in, w, bias_tiles):
    n_seq, t, d = x_in.shape
    tm = MIX_TM
    tps = t // tm
    ffn_tps = t // FFN_TM
    x = x_in.reshape(n_seq * t, d)
    outs = {k: [] for k in ("k_diff", "v_diff", "k_fox", "v_fox", "logf", "conv_a", "conv_ffn")}
    for l in range(DEPTH):
        g_mix = w["norm_mix_g"][l][None]
        if l % 2 == 0:
            e = l // 2
            za, zk, zv, zb = _proj(x, g_mix, w["w_in_even"][e], EVEN_PLAN, PROJ_TM, PROJ_TN)
            lam_params = [w[n][e][None] for n in ("lam_q1", "lam_k1", "lam_q2", "lam_k2")]
            ob = _diff_attn(zb, bias_tiles, lam_params, w["subln_g"][e][None], n_seq, t, DIFF_TQ,
                            _lam_init(l))
            prev = jnp.zeros((n_seq, CONV_W - 1, A_WIDTH), F32)
            x, st = _mix_even(x, za, ob, w["conv_a_w"][e], prev, w["w_out_even"][e], tm, tps, 1)
            outs["k_diff"].append(zk.reshape(n_seq, t, DIFF_HEADS, HEAD_W))
            outs["v_diff"].append(zv.reshape(n_seq, t, DIFF_HEADS, HEAD_W))
            outs["conv_a"].append(st[tps - 1::tps])
        else:
            o = l // 2
            zk, zv, zb, f = _proj(x, g_mix, w["w_in_odd"][o], ODD_PLAN, PROJ_TM, PROJ_TN,
                                  w_tail=w["w_forget"][o])
            f_t = f[:, :FOX_HEADS].reshape(n_seq, t, FOX_HEADS)
            f_t = f_t.transpose(0, 2, 1).reshape(n_seq * FOX_HEADS, t)
            b_col = jnp.tile(w["b_forget"][o], n_seq)[:, None]
            logf_t, c_t = _gates_prompt(f_t, b_col)
            c_row = c_t.reshape(n_seq * FOX_HEADS, t // FOX_TQ, 1, FOX_TQ)
            oc = _fox_attn(zb, c_row, n_seq, t, FOX_TQ)
            x = _mix_odd(x, oc, w["w_out_odd"][o], tm)
            outs["k_fox"].append(zk.reshape(n_seq, t, FOX_HEADS, HEAD_W))
            outs["v_fox"].append(zv.reshape(n_seq, t, FOX_HEADS, HEAD_W))
            outs["logf"].append(logf_t.reshape(n_seq, FOX_HEADS, t).transpose(0, 2, 1))
        prev_f = jnp.zeros((n_seq, CONV_W - 1, D_FF), F32)
        x, st = _ffn(x, w["norm_ffn_g"][l][None], w["w_gate_ffn"][l], w["w_up_ffn"][l],
                     w["conv_ffn_w"][l], prev_f, w["w_down_ffn"][l], w["norm_final_g"][None],
                     FFN_TM, FFN_TF, ffn_tps, 1, l == DEPTH - 1)
        outs["conv_ffn"].append(st[ffn_tps - 1::ffn_tps])
    return x.reshape(n_seq, t, d), {k: jnp.stack(v) for k, v in outs.items()}


def _time_major_state(s):
    return s.transpose(1, 0, 2).reshape(1, -1, s.shape[-1])


def _state_from_time_major(s, n_seq):
    return s.reshape(CONV_W - 1, n_seq, s.shape[-1]).transpose(1, 0, 2)


def _run_sample(x_in, past, w):
    (cache_k_diff, cache_v_diff, cache_k_fox, cache_v_fox, cache_logf_fox,
     state_conv_a, state_conv_ffn, page_table) = past
    n_seq, t, d = x_in.shape
    m = n_seq * t
    past_len = page_table.shape[1] * PAGE_SIZE
    ck_diff = cache_k_diff.reshape(cache_k_diff.shape[:2] + (PAGE_SIZE * DIFF_HEADS, HEAD_W))
    cv_diff = cache_v_diff.reshape(cache_v_diff.shape[:2] + (PAGE_SIZE * DIFF_HEADS, HEAD_W))
    ck_fox = cache_k_fox.reshape(cache_k_fox.shape[:2] + (PAGE_SIZE * FOX_HEADS, HEAD_W))
    cv_fox = cache_v_fox.reshape(cache_v_fox.shape[:2] + (PAGE_SIZE * FOX_HEADS, HEAD_W))
    clf_fox = cache_logf_fox.reshape(cache_logf_fox.shape[:2] + (1, PAGE_SIZE * FOX_HEADS))
    bfar, blast, bnew = _sample_diff_tables(w["rel_bias"], past_len, t, NEW_PAD)
    mfar, mnew = _sample_fox_masks(t, NEW_PAD)

    def per_seq(a, heads):
        return a.reshape(t, n_seq, heads, HEAD_W).transpose(1, 0, 2, 3)

    def pad_new(a):
        a = a.reshape(n_seq, -1, HEAD_W)
        return jnp.pad(a, ((0, 0), (0, NEW_PAD - a.shape[1]), (0, 0)))

    x = x_in.transpose(1, 0, 2).reshape(m, d)
    outs = {k: [] for k in ("k_diff", "v_diff", "k_fox", "v_fox", "logf", "conv_a", "conv_ffn")}
    for l in range(DEPTH):
        g_mix = w["norm_mix_g"][l][None]
        if l % 2 == 0:
            e = l // 2
            z, zk, zv, zb = _proj(x, g_mix, w["w_in_even"][e], EVEN_PLAN, m, PROJ_TN)
            q = per_seq(zb[:, :DIFF_QK_WIDTH], DIFF_HEADS)
            k = per_seq(zk, DIFF_HEADS)
            v = per_seq(zv, DIFF_HEADS)
            q_rows = q.transpose(0, 2, 1, 3).reshape(n_seq, DIFF_HEADS * t, HEAD_W)
            q_rows = jnp.concatenate([q_rows, q_rows], axis=1)
            lam_params = [w[n][e][None] for n in ("lam_q1", "lam_k1", "lam_q2", "lam_k2")]
            o_dec = _dec_diff(page_table, q_rows, pad_new(k), pad_new(v), bfar, blast, bnew,
                              lam_params, w["subln_g"][e][None], ck_diff, cv_diff, e,
                              DIFF_PAGES_PER_STEP, _lam_init(l))
            ob = o_dec.reshape(n_seq, DIFF_HEADS, t, HEAD_W).transpose(2, 0, 1, 3)
            ob = ob.reshape(m, B_WIDTH).astype(BF16)
            x, st = _mix_even(x, z, ob, w["conv_a_w"][e], _time_major_state(state_conv_a[e]),
                              w["w_out_even"][e], m, 1, n_seq)
            outs["k_diff"].append(k)
            outs["v_diff"].append(v)
            outs["conv_a"].append(_state_from_time_major(st, n_seq))
        else:
            o = l // 2
            zk, zv, zb, f = _proj(x, g_mix, w["w_in_odd"][o], ODD_PLAN, m, PROJ_TN,
                                  w_tail=w["w_forget"][o])
            q = per_seq(zb[:, :FOX_WIDTH], FOX_HEADS)
            k = per_seq(zk, FOX_HEADS)
            v = per_seq(zv, FOX_HEADS)
            logf, cn = _gates_sample(f[:, :FOX_HEADS], w["b_forget"][o][None], n_seq, t)
            cn = cn.reshape(t, n_seq, FOX_HEADS)
            cn_col = cn.transpose(1, 2, 0).reshape(n_seq, FOX_HEADS * t, 1)
            cn_row = cn.transpose(1, 0, 2).reshape(n_seq, 1, t * FOX_HEADS)
            cn_row = jnp.pad(cn_row, ((0, 0), (0, 0), (0, NEW_PAD - t * FOX_HEADS)))
            q_rows = q.transpose(0, 2, 1, 3).reshape(n_seq, FOX_HEADS * t, HEAD_W)
            o_dec = _dec_fox(page_table, q_rows, pad_new(k), pad_new(v), cn_col, cn_row, mfar, mnew,
                             ck_fox, cv_fox, clf_fox, o, FOX_PAGES_PER_STEP)
            oc = o_dec.reshape(n_seq, FOX_HEADS, t, HEAD_W).transpose(2, 0, 1, 3)
            oc = oc.reshape(m, FOX_WIDTH).astype(BF16)
            x = _mix_odd(x, oc, w["w_out_odd"][o], m)
            outs["k_fox"].append(k)
            outs["v_fox"].append(v)
            outs["logf"].append(logf.reshape(t, n_seq, FOX_HEADS).transpose(1, 0, 2))
        x, st = _ffn(x, w["norm_ffn_g"][l][None], w["w_gate_ffn"][l], w["w_up_ffn"][l],
                     w["conv_ffn_w"][l], _time_major_state(state_conv_ffn[l]), w["w_down_ffn"][l],
                     w["norm_final_g"][None], m, FFN_TF, 1, n_seq, l == DEPTH - 1)
        outs["conv_ffn"].append(_state_from_time_major(st, n_seq))
    y = x.reshape(t, n_seq, d).transpose(1, 0, 2)
    return y, {k: jnp.stack(v) for k, v in outs.items()}


def kernel(x_prompt, x_sample, cache_k_diff, cache_v_diff, cache_k_fox, cache_v_fox, cache_logf_fox,
           state_conv_a, state_conv_ffn, page_table,
           norm_mix_g, norm_ffn_g, norm_final_g, w_in_even, w_out_even, conv_a_w,
           lam_q1, lam_k1, lam_q2, lam_k2, subln_g, rel_bias,
           w_in_odd, b_forget, w_out_odd, w_gate_ffn, w_up_ffn, conv_ffn_w, w_down_ffn):
    w = _prep_weights(dict(
        norm_mix_g=norm_mix_g, norm_ffn_g=norm_ffn_g, norm_final_g=norm_final_g,
        w_in_even=w_in_even, w_out_even=w_out_even, conv_a_w=conv_a_w,
        lam_q1=lam_q1, lam_k1=lam_k1, lam_q2=lam_q2, lam_k2=lam_k2, subln_g=subln_g,
        rel_bias=rel_bias, w_in_odd=w_in_odd, b_forget=b_forget, w_out_odd=w_out_odd,
        w_gate_ffn=w_gate_ffn, w_up_ffn=w_up_ffn, conv_ffn_w=conv_ffn_w, w_down_ffn=w_down_ffn))
    bias_tiles = _prompt_bias_tiles(rel_bias, DIFF_TQ)
    y_p, sp = _run_prompt(x_prompt, w, bias_tiles)
    past = (cache_k_diff, cache_v_diff, cache_k_fox, cache_v_fox, cache_logf_fox,
            state_conv_a, state_conv_ffn, page_table)
    y_s, ss = _run_sample(x_sample, past, w)
    order = ("k_diff", "v_diff", "k_fox", "v_fox", "logf", "conv_a", "conv_ffn")
    return (y_p, y_s) + tuple(sp[k] for k in order) + tuple(ss[k] for k in order)
```

```python
import functools
import math

import jax
import jax.numpy as jnp
from jax import lax
from jax.experimental import pallas as pl
from jax.experimental.pallas import tpu as pltpu

D_MODEL = 1024
DEPTH = 4
PAGE_SIZE = 128
CONV_W = 3
A_WIDTH = D_MODEL // 2
DIFF_HEADS = 4
DIFF_QK_DIM = D_MODEL // 16
DIFF_V_DIM = 2 * DIFF_QK_DIM
HEAD_W = 128
DIFF_QK_WIDTH = DIFF_HEADS * 2 * DIFF_QK_DIM
B_WIDTH = DIFF_HEADS * DIFF_V_DIM
EVEN_IN = 3 * A_WIDTH + 2 * DIFF_QK_WIDTH + B_WIDTH
FOX_HEADS = 8
FOX_HEAD_DIM = D_MODEL // FOX_HEADS
FOX_WIDTH = FOX_HEADS * FOX_HEAD_DIM
NUM_BUCKETS = 32
MAX_DISTANCE = 128
D_FF = (11 * D_MODEL) // 4
EPS = 1e-6
NEG = -1e30
LOG2E = 1.4426950408889634
FOX_HEADS_PER_STEP = 2
DIFF_HEADS_PER_STEP = 2

F32 = jnp.float32
BF16 = jnp.bfloat16
MIB = 1024 * 1024
NT_DIMS = (((1,), (1,)), ((), ()))


def _cparams(n_axes, vmem_mib):
    return pltpu.CompilerParams(dimension_semantics=("arbitrary",) * n_axes,
                                vmem_limit_bytes=vmem_mib * MIB)


def _rms(x, g):
    ms = jnp.mean(x * x, axis=-1, keepdims=True)
    return x * lax.rsqrt(ms + EPS) * g


def _halo_rows(shift):
    return -(-(2 * shift) // 8) * 8


def _proj_kernel(x_ref, g_ref, w_ref, *rest, plan, tn, has_tail):
    outs = rest[has_tail:]
    tm = x_ref.shape[0]
    h = _rms(x_ref[...], g_ref[...]).astype(BF16)
    if has_tail:
        outs[-1][...] = jnp.dot(h, rest[0][...], preferred_element_type=F32)
    heads_per_tile = tn // HEAD_W
    for c in range(w_ref.shape[1] // tn):
        z = jnp.dot(h, w_ref[:, c * tn:(c + 1) * tn], preferred_element_type=F32)
        for o_ref, (dtype, first, count, by_head) in zip(outs, plan):
            if not first <= c < first + count:
                continue
            if by_head:
                n_heads = count * heads_per_tile
                for hh in range(heads_per_tile):
                    hd = (c - first) * heads_per_tile + hh
                    o_ref[pl.ds(hd, tm, stride=n_heads), :] = (
                        z[:, hh * HEAD_W:(hh + 1) * HEAD_W].astype(dtype))
            else:
                o_ref[:, (c - first) * tn:(c - first + 1) * tn] = z.astype(dtype)


def _proj(x, g, w, plan, tm, tn, w_tail=None):
    m, d = x.shape
    has_tail = w_tail is not None
    in_specs = [pl.BlockSpec((tm, d), lambda i: (i, 0)),
                pl.BlockSpec((1, d), lambda i: (0, 0)),
                pl.BlockSpec(w.shape, lambda i: (0, 0))]
    out_specs, out_shape = [], []
    for dtype, _, count, by_head in plan:
        heads = count * tn // HEAD_W
        rows, cols = (heads, HEAD_W) if by_head else (1, count * tn)
        out_specs.append(pl.BlockSpec((tm * rows, cols), lambda i: (i, 0)))
        out_shape.append(jax.ShapeDtypeStruct((m * rows, cols), dtype))
    args = [x, g, w]
    if has_tail:
        in_specs.append(pl.BlockSpec(w_tail.shape, lambda i: (0, 0)))
        out_specs.append(pl.BlockSpec((tm, w_tail.shape[1]), lambda i: (i, 0)))
        out_shape.append(jax.ShapeDtypeStruct((m, w_tail.shape[1]), F32))
        args.append(w_tail)
    return pl.pallas_call(
        functools.partial(_proj_kernel, plan=tuple(plan), tn=tn, has_tail=has_tail),
        grid=(m // tm,),
        in_specs=in_specs,
        out_specs=out_specs,
        out_shape=out_shape,
        compiler_params=_cparams(1, 48),
        name="proj",
    )(*args)


def _softmax_rows(t, m_ref, l_ref, acc_ref, idx):
    m_old = m_ref[idx]
    m_new = jnp.maximum(m_old, jnp.max(t, axis=-1, keepdims=True))
    alpha = jnp.exp2(m_old - m_new)
    p = jnp.exp2(t - jnp.tile(m_new, (1, t.shape[1] // HEAD_W)))
    m_ref[idx] = m_new
    l_ref[idx] = alpha * l_ref[idx] + jnp.sum(p, axis=-1, keepdims=True)
    acc_ref[idx] = alpha * acc_ref[idx]
    return p.astype(BF16)


def _diff_lambda(lq1_ref, lk1_ref, lq2_ref, lk2_ref, lam_init):
    return (jnp.exp(jnp.sum(lq1_ref[...] * lk1_ref[...], axis=-1, keepdims=True))
            - jnp.exp(jnp.sum(lq2_ref[...] * lk2_ref[...], axis=-1, keepdims=True)) + lam_init)


def _diff_attn_kernel(q_ref, k_ref, v_ref, bias_ref, lq1_ref, lk1_ref, lq2_ref, lk2_ref, sg_ref,
                      o_ref, m_ref, l_ref, acc_ref, *, tq, lam_init):
    qi = pl.program_id(2)
    lane = lax.broadcasted_iota(jnp.int32, (tq, HEAD_W), 1)
    heads = [slice(hh * HEAD_W, (hh + 1) * HEAD_W) for hh in range(DIFF_HEADS_PER_STEP)]
    q_maps = []
    for hs in heads:
        qs = q_ref[:, hs].astype(F32) * (DIFF_QK_DIM ** -0.5)
        q_maps.append((jnp.where(lane < DIFF_QK_DIM, qs, 0.0).astype(BF16),
                       jnp.where(lane >= DIFF_QK_DIM, qs, 0.0).astype(BF16)))
    m_ref[...] = jnp.full(m_ref.shape, NEG, F32)
    l_ref[...] = jnp.zeros(l_ref.shape, F32)
    acc_ref[...] = jnp.zeros(acc_ref.shape, F32)

    def step(kt, kind):
        rows = pl.ds(pl.multiple_of(kt * tq, tq), tq)
        for hh, hs in enumerate(heads):
            k = k_ref[rows, hs]
            v = v_ref[rows, hs]
            for mi in range(2):
                t = lax.dot_general(q_maps[hh][mi], k, NT_DIMS, preferred_element_type=F32) * LOG2E
                if kind is not None:
                    t = t + bias_ref[hh, kind] * LOG2E
                p = _softmax_rows(t, m_ref, l_ref, acc_ref, 2 * hh + mi)
                acc_ref[2 * hh + mi] += jnp.dot(p, v, preferred_element_type=F32)

    def far_body(kt, carry):
        step(kt, None)
        return carry

    lax.fori_loop(0, jnp.maximum(qi - 1, 0), far_body, 0)

    @pl.when(qi >= 1)
    def _():
        step(qi - 1, 1)

    step(qi, 0)

    lam = _diff_lambda(lq1_ref, lk1_ref, lq2_ref, lk2_ref, lam_init)
    for hh, hs in enumerate(heads):
        o = acc_ref[2 * hh] / l_ref[2 * hh] - lam * (acc_ref[2 * hh + 1] / l_ref[2 * hh + 1])
        o_ref[:, hs] = (_rms(o, sg_ref[...]) * (1.0 - lam_init)).astype(BF16)


def _diff_attn(zb, bias, lam_params, sg, n_seq, t, tq, lam_init):
    m = zb.shape[0]
    nq = t // tq
    hps = DIFF_HEADS_PER_STEP
    groups = DIFF_HEADS // hps
    width = hps * HEAD_W
    small = pl.BlockSpec((1, DIFF_QK_DIM), lambda b, h, i: (0, 0))
    return pl.pallas_call(
        functools.partial(_diff_attn_kernel, tq=tq, lam_init=lam_init),
        grid=(n_seq, groups, nq),
        in_specs=[pl.BlockSpec((tq, width), lambda b, h, i: (b * nq + i, h)),
                  pl.BlockSpec((t, width), lambda b, h, i: (b, groups + h)),
                  pl.BlockSpec((t, width), lambda b, h, i: (b, 2 * groups + h)),
                  pl.BlockSpec((hps, 2, tq, tq), lambda b, h, i: (h, 0, 0, 0)),
                  small, small, small, small,
                  pl.BlockSpec((1, DIFF_V_DIM), lambda b, h, i: (0, 0))],
        out_specs=pl.BlockSpec((tq, width), lambda b, h, i: (b * nq + i, h)),
        out_shape=jax.ShapeDtypeStruct((m, B_WIDTH), BF16),
        scratch_shapes=[pltpu.VMEM((2 * hps, tq, HEAD_W), F32)] * 3,
        compiler_params=_cparams(3, 40),
        name="diff_attn",
    )(zb, zb, zb, bias, *lam_params, sg)


def _fox_attn_kernel(q_ref, k_ref, v_ref, ck_ref, o_ref, m_ref, l_ref, acc_ref, *, tq):
    qi = pl.program_id(2)
    scale2 = (FOX_HEAD_DIM ** -0.5) * LOG2E
    row = lax.broadcasted_iota(jnp.int32, (tq, tq), 0)
    col = lax.broadcasted_iota(jnp.int32, (tq, tq), 1)
    m_ref[...] = jnp.full(m_ref.shape, NEG, F32)
    l_ref[...] = jnp.zeros(l_ref.shape, F32)
    acc_ref[...] = jnp.zeros(acc_ref.shape, F32)
    heads = [slice(hh * HEAD_W, (hh + 1) * HEAD_W) for hh in range(FOX_HEADS_PER_STEP)]
    qs = [q_ref[:, hs] for hs in heads]
    cqs = [jnp.sum(jnp.where(row == col, ck_ref[hh, qi], 0.0), axis=-1, keepdims=True) * LOG2E
           for hh in range(FOX_HEADS_PER_STEP)]

    def step(kt, diagonal):
        rows = pl.ds(pl.multiple_of(kt * tq, tq), tq)
        for hh, hs in enumerate(heads):
            s = lax.dot_general(qs[hh], k_ref[rows, hs], NT_DIMS, preferred_element_type=F32)
            t = s * scale2 + (cqs[hh] - ck_ref[hh, kt] * LOG2E)
            if diagonal:
                t = jnp.where(row >= col, t, NEG)
            p = _softmax_rows(t, m_ref, l_ref, acc_ref, hh)
            acc_ref[hh] += jnp.dot(p, v_ref[rows, hs], preferred_element_type=F32)

    def far_body(kt, carry):
        step(kt, False)
        return carry

    lax.fori_loop(0, qi, far_body, 0)
    step(qi, True)
    for hh, hs in enumerate(heads):
        o_ref[:, hs] = (acc_ref[hh] / l_ref[hh]).astype(BF16)


def _fox_attn(zb, c_row, n_seq, t, tq):
    m = zb.shape[0]
    nq = t // tq
    hps = FOX_HEADS_PER_STEP
    groups = FOX_HEADS // hps
    width = hps * HEAD_W
    return pl.pallas_call(
        functools.partial(_fox_attn_kernel, tq=tq),
        grid=(n_seq, groups, nq),
        in_specs=[pl.BlockSpec((tq, width), lambda b, h, i: (b * nq + i, h)),
                  pl.BlockSpec((t, width), lambda b, h, i: (b, groups + h)),
                  pl.BlockSpec((t, width), lambda b, h, i: (b, 2 * groups + h)),
                  pl.BlockSpec((hps, nq, 1, tq), lambda b, h, i: (b * groups + h, 0, 0, 0))],
        out_specs=pl.BlockSpec((tq, width), lambda b, h, i: (b * nq + i, h)),
        out_shape=jax.ShapeDtypeStruct((m, FOX_WIDTH), BF16),
        scratch_shapes=[pltpu.VMEM((hps, tq, HEAD_W), F32)] * 3,
        compiler_params=_cparams(3, 40),
        name="fox_attn",
    )(zb, zb, zb, c_row)


def _log_sigmoid(x):
    return -(jnp.maximum(-x, 0.0) + jnp.log1p(jnp.exp(-jnp.abs(x))))


def _split3(x):
    hi = x.astype(BF16)
    r1 = x - hi.astype(F32)
    mid = r1.astype(BF16)
    lo = (r1 - mid.astype(F32)).astype(BF16)
    return hi, mid, lo


def _gates_prompt_kernel(f_ref, b_ref, logf_ref, c_ref, *, chunk):
    rows, t = f_ref.shape
    r = lax.broadcasted_iota(jnp.int32, (chunk, chunk), 0)
    cidx = lax.broadcasted_iota(jnp.int32, (chunk, chunk), 1)
    tri = jnp.where(r <= cidx, 1.0, 0.0).astype(BF16)
    carry = jnp.zeros((rows, 1), F32)
    for ci in range(t // chunk):
        sl = slice(ci * chunk, (ci + 1) * chunk)
        logf = _log_sigmoid(f_ref[:, sl] + b_ref[...])
        logf_ref[:, sl] = logf
        y = carry
        for part in _split3(logf):
            y = y + jnp.dot(part, tri, preferred_element_type=F32)
        c_ref[:, sl] = y
        carry = y[:, chunk - 1:chunk]


def _gates_prompt(f_t, b_col):
    rows, t = f_t.shape
    full = pl.BlockSpec((rows, t), lambda i: (0, 0))
    return pl.pallas_call(
        functools.partial(_gates_prompt_kernel, chunk=256),
        grid=(1,),
        in_specs=[full, pl.BlockSpec((rows, 1), lambda i: (0, 0))],
        out_specs=[full, full],
        out_shape=[jax.ShapeDtypeStruct((rows, t), F32)] * 2,
        compiler_params=_cparams(1, 32),
        name="gates_prompt",
    )(f_t, b_col)


def _gates_sample_kernel(f_ref, b_ref, logf_ref, c_ref, *, n_seq, t):
    logf = _log_sigmoid(f_ref[...] + b_ref[...])
    logf_ref[...] = logf
    run = logf[0:n_seq]
    c_ref[0:n_seq, :] = run
    for ti in range(1, t):
        run = run + logf[ti * n_seq:(ti + 1) * n_seq]
        c_ref[ti * n_seq:(ti + 1) * n_seq, :] = run


def _gates_sample(f, b_row, n_seq, t):
    m, h = f.shape
    full = pl.BlockSpec((m, h), lambda i: (0, 0))
    return pl.pallas_call(
        functools.partial(_gates_sample_kernel, n_seq=n_seq, t=t),
        grid=(1,),
        in_specs=[full, pl.BlockSpec((1, h), lambda i: (0, 0))],
        out_specs=[full, full],
        out_shape=[jax.ShapeDtypeStruct((m, h), F32)] * 2,
        compiler_params=_cparams(1, 32),
        name="gates_sample",
    )(f, b_row)


def _conv3(buf_ref, cw_ref, cur, tm, shift, halo):
    y = buf_ref[pl.ds(halo - 2 * shift, tm), :] * cw_ref[0:1, :]
    y = y + buf_ref[pl.ds(halo - shift, tm), :] * cw_ref[1:2, :]
    return y + cur * cw_ref[2:3, :]


def _mix_even_kernel(x_ref, ab_ref, ac_ref, ah_ref, ob_ref, cw_ref, prev_ref, w_ref,
                     xo_ref, st_ref, ubuf, *, tm, tps, shift):
    halo = _halo_rows(shift)
    first = (pl.program_id(0) % tps) == 0

    @pl.when(first)
    def _():
        ubuf[pl.ds(halo - 2 * shift, 2 * shift), :] = prev_ref[...]

    @pl.when(jnp.logical_not(first))
    def _():
        ubuf[pl.ds(halo - 2 * shift, 2 * shift), :] = ubuf[pl.ds(halo + tm - 2 * shift, 2 * shift), :]

    u = ac_ref[...] * ah_ref[...]
    ubuf[pl.ds(halo, tm), :] = u
    ya = ab_ref[...] * _conv3(ubuf, cw_ref, u, tm, shift, halo)
    st_ref[...] = ubuf[pl.ds(halo + tm - 2 * shift, 2 * shift), :]
    y = jnp.dot(ya.astype(BF16), w_ref[0:A_WIDTH, :], preferred_element_type=F32)
    y = y + jnp.dot(ob_ref[...], w_ref[A_WIDTH:, :], preferred_element_type=F32)
    xo_ref[...] = x_ref[...] + y


def _mix_even(x, z, ob, cw, prev, w, tm, tps, shift):
    m, d = x.shape
    return pl.pallas_call(
        functools.partial(_mix_even_kernel, tm=tm, tps=tps, shift=shift),
        grid=(m // tm,),
        in_specs=[pl.BlockSpec((tm, d), lambda i: (i, 0)),
                  pl.BlockSpec((tm, A_WIDTH), lambda i: (i, 0)),
                  pl.BlockSpec((tm, A_WIDTH), lambda i: (i, 1)),
                  pl.BlockSpec((tm, A_WIDTH), lambda i: (i, 2)),
                  pl.BlockSpec((tm, B_WIDTH), lambda i: (i, 0)),
                  pl.BlockSpec((CONV_W, A_WIDTH), lambda i: (0, 0)),
                  pl.BlockSpec((None, 2 * shift, A_WIDTH), lambda i: (i // tps, 0, 0)),
                  pl.BlockSpec((A_WIDTH + B_WIDTH, d), lambda i: (0, 0))],
        out_specs=[pl.BlockSpec((tm, d), lambda i: (i, 0)),
                   pl.BlockSpec((None, 2 * shift, A_WIDTH), lambda i: (i, 0, 0))],
        out_shape=[jax.ShapeDtypeStruct((m, d), F32),
                   jax.ShapeDtypeStruct((m // tm, 2 * shift, A_WIDTH), F32)],
        scratch_shapes=[pltpu.VMEM((_halo_rows(shift) + tm, A_WIDTH), F32)],
        compiler_params=_cparams(1, 48),
        name="mix_even",
    )(x, z, z, z, ob, cw, prev, w)


def _mix_odd_kernel(x_ref, o_ref, w_ref, xo_ref):
    xo_ref[...] = x_ref[...] + jnp.dot(o_ref[...], w_ref[...], preferred_element_type=F32)


def _mix_odd(x, o, w, tm):
    m, d = x.shape
    return pl.pallas_call(
        _mix_odd_kernel,
        grid=(m // tm,),
        in_specs=[pl.BlockSpec((tm, d), lambda i: (i, 0)),
                  pl.BlockSpec((tm, FOX_WIDTH), lambda i: (i, 0)),
                  pl.BlockSpec((FOX_WIDTH, d), lambda i: (0, 0))],
        out_specs=pl.BlockSpec((tm, d), lambda i: (i, 0)),
        out_shape=jax.ShapeDtypeStruct((m, d), F32),
        compiler_params=_cparams(1, 40),
        name="mix_odd",
    )(x, o, w)


def _ffn_kernel(x_ref, g_ref, wg_ref, wu_ref, cw_ref, prev_ref, wd_ref, gf_ref,
                xo_ref, st_ref, h_ref, acc_ref, gbuf, carry_ref, *, tm, tps, shift, final_norm):
    halo = _halo_rows(shift)
    i = pl.program_id(0)
    j = pl.program_id(1)
    first = (i % tps) == 0

    @pl.when(j == 0)
    def _():
        h_ref[...] = _rms(x_ref[...], g_ref[...]).astype(BF16)
        acc_ref[...] = jnp.zeros(acc_ref.shape, F32)

    @pl.when(first)
    def _():
        gbuf[pl.ds(halo - 2 * shift, 2 * shift), :] = prev_ref[...]

    @pl.when(jnp.logical_not(first))
    def _():
        gbuf[pl.ds(halo - 2 * shift, 2 * shift), :] = carry_ref[j]

    h = h_ref[...]
    gate = jnp.dot(h, wg_ref[...], preferred_element_type=F32)
    up = jnp.dot(h, wu_ref[...], preferred_element_type=F32)
    gbuf[pl.ds(halo, tm), :] = gate
    tail = gbuf[pl.ds(halo + tm - 2 * shift, 2 * shift), :]
    carry_ref[j] = tail
    st_ref[...] = tail
    gc = _conv3(gbuf, cw_ref, gate, tm, shift, halo)
    act = (gc * (0.5 * jnp.tanh(0.5 * gc) + 0.5)) * up
    acc_ref[...] += jnp.dot(act.astype(BF16), wd_ref[...], preferred_element_type=F32)

    @pl.when(j == pl.num_programs(1) - 1)
    def _():
        xn = x_ref[...] + acc_ref[...]
        xo_ref[...] = _rms(xn, gf_ref[...]) if final_norm else xn


def _ffn(x, g, wg, wu, cw, prev, wd, gf, tm, tf, tps, shift, final_norm):
    m, d = x.shape
    f = wg.shape[1]
    return pl.pallas_call(
        functools.partial(_ffn_kernel, tm=tm, tps=tps, shift=shift, final_norm=final_norm),
        grid=(m // tm, f // tf),
        in_specs=[pl.BlockSpec((tm, d), lambda i, j: (i, 0)),
                  pl.BlockSpec((1, d), lambda i, j: (0, 0)),
                  pl.BlockSpec((d, tf), lambda i, j: (0, j)),
                  pl.BlockSpec((d, tf), lambda i, j: (0, j)),
                  pl.BlockSpec((CONV_W, tf), lambda i, j: (0, j)),
                  pl.BlockSpec((None, 2 * shift, tf), lambda i, j: (i // tps, 0, j)),
                  pl.BlockSpec((tf, d), lambda i, j: (j, 0)),
                  pl.BlockSpec((1, d), lambda i, j: (0, 0))],
        out_specs=[pl.BlockSpec((tm, d), lambda i, j: (i, 0)),
                   pl.BlockSpec((None, 2 * shift, tf), lambda i, j: (i, 0, j))],
        out_shape=[jax.ShapeDtypeStruct((m, d), F32),
                   jax.ShapeDtypeStruct((m // tm, 2 * shift, f), F32)],
        scratch_shapes=[pltpu.VMEM((tm, d), BF16), pltpu.VMEM((tm, d), F32),
                        pltpu.VMEM((_halo_rows(shift) + tm, tf), F32),
                        pltpu.VMEM((f // tf, 2 * shift, tf), F32)],
        compiler_params=_cparams(2, 56),
        name="ffn",
    )(x, g, wg, wu, cw, prev, wd, gf)


def _online_update(s, pv_fn, m_ref, l_ref, acc_ref):
    m_old = m_ref[...]
    m_new = jnp.maximum(m_old, jnp.max(s, axis=-1, keepdims=True))
    alpha = jnp.exp(m_old - m_new)
    p = jnp.exp(s - m_new)
    l_ref[...] = alpha * l_ref[...] + jnp.sum(p, axis=-1, keepdims=True)
    acc_ref[...] = alpha * acc_ref[...] + pv_fn(p.astype(BF16))
    m_ref[...] = m_new


def _dec_diff_kernel(pt_ref, q_ref, kn_ref, vn_ref, bfar_ref, blast_ref, bnew_ref,
                     lq1_ref, lk1_ref, lq2_ref, lk2_ref, sg_ref, *rest, n_pages_step, lam_init):
    del pt_ref
    g_n = n_pages_step
    kp, vp = rest[:g_n], rest[g_n:2 * g_n]
    o_ref, m_ref, l_ref, acc_ref = rest[2 * g_n:]
    j = pl.program_id(1)
    last = pl.num_programs(1) - 1
    rows = q_ref.shape[0]
    krows = kp[0].shape[0]

    @pl.when(j == 0)
    def _():
        m_ref[...] = jnp.full(m_ref.shape, NEG, F32)
        l_ref[...] = jnp.zeros(l_ref.shape, F32)
        acc_ref[...] = jnp.zeros(acc_ref.shape, F32)

    lane = lax.broadcasted_iota(jnp.int32, (rows, HEAD_W), 1)
    row = lax.broadcasted_iota(jnp.int32, (rows, HEAD_W), 0)
    keep = (lane // DIFF_QK_DIM) == (row // (rows // 2))
    q = jnp.where(keep, q_ref[...].astype(F32) * (DIFF_QK_DIM ** -0.5), 0.0).astype(BF16)

    scores = []
    for g in range(g_n):
        s = lax.dot_general(q, kp[g][...].astype(BF16), NT_DIMS, preferred_element_type=F32)
        if g == g_n - 1:
            bias = jnp.where(j == last, blast_ref[...], bfar_ref[...])
        else:
            bias = bfar_ref[...]
        scores.append(s + bias)

    def pv_pages(p):
        out = jnp.dot(p[:, 0:krows], vp[0][...].astype(BF16), preferred_element_type=F32)
        for g in range(1, g_n):
            out = out + jnp.dot(p[:, g * krows:(g + 1) * krows], vp[g][...].astype(BF16),
                                preferred_element_type=F32)
        return out

    _online_update(jnp.concatenate(scores, axis=1), pv_pages, m_ref, l_ref, acc_ref)

    @pl.when(j == last)
    def _():
        s = lax.dot_general(q, kn_ref[...].astype(BF16), NT_DIMS, preferred_element_type=F32)
        vn = vn_ref[...].astype(BF16)
        _online_update(s + bnew_ref[...], lambda p: jnp.dot(p, vn, preferred_element_type=F32),
                       m_ref, l_ref, acc_ref)
        a = acc_ref[...] / l_ref[...]
        lam = _diff_lambda(lq1_ref, lk1_ref, lq2_ref, lk2_ref, lam_init)
        o = a[0:rows // 2] - lam * a[rows // 2:rows]
        o_ref[...] = _rms(o, sg_ref[...]) * (1.0 - lam_init)


def _dec_diff(page_table, q, kn, vn, bfar, blast, bnew, lam_params, sg, cache_k, cache_v, layer,
              n_pages_step, lam_init):
    n_seq, rows, _ = q.shape
    n_pages = page_table.shape[1]
    g_n = n_pages_step
    krows = cache_k.shape[2]

    def page_spec(g):
        return pl.BlockSpec((None, None, krows, HEAD_W),
                            lambda b, j, pt: (layer, pt[b, j * g_n + g], 0, 0))

    const2 = lambda b, j, pt: (0, 0)
    per_seq = lambda b, j, pt: (b, 0, 0)
    small = pl.BlockSpec((1, DIFF_QK_DIM), const2)
    grid_spec = pltpu.PrefetchScalarGridSpec(
        num_scalar_prefetch=1,
        grid=(n_seq, n_pages // g_n),
        in_specs=[pl.BlockSpec((None, rows, HEAD_W), per_seq),
                  pl.BlockSpec((None,) + kn.shape[1:], per_seq),
                  pl.BlockSpec((None,) + vn.shape[1:], per_seq),
                  pl.BlockSpec(bfar.shape, const2),
                  pl.BlockSpec(blast.shape, const2),
                  pl.BlockSpec(bnew.shape, const2),
                  small, small, small, small,
                  pl.BlockSpec((1, DIFF_V_DIM), const2)]
                 + [page_spec(g) for g in range(g_n)] * 2,
        out_specs=pl.BlockSpec((None, rows // 2, HEAD_W), per_seq),
        scratch_shapes=[pltpu.VMEM((rows, 1), F32), pltpu.VMEM((rows, 1), F32),
                        pltpu.VMEM((rows, HEAD_W), F32)],
    )
    return pl.pallas_call(
        functools.partial(_dec_diff_kernel, n_pages_step=g_n, lam_init=lam_init),
        grid_spec=grid_spec,
        out_shape=jax.ShapeDtypeStruct((n_seq, rows // 2, HEAD_W), F32),
        compiler_params=_cparams(2, 48),
        name="dec_diff",
    )(page_table, q, kn, vn, bfar, blast, bnew, *lam_params, sg,
      *([cache_k] * g_n), *([cache_v] * g_n))


def _suffix_by_head(x, n_heads):
    width = x.shape[1]
    lane = lax.broadcasted_iota(jnp.int32, x.shape, 1)
    incl = x
    tot = x
    sh = n_heads
    while sh < width:
        shifted = pltpu.roll(incl, width - sh, axis=1)
        incl = incl + jnp.where(lane < width - sh, shifted, 0.0)
        tot = tot + pltpu.roll(tot, sh, axis=1)
        sh *= 2
    return incl, tot


def _dec_fox_kernel(pt_ref, q_ref, kn_ref, vn_ref, cncol_ref, cnrow_ref, mfar_ref, mnew_ref,
                    *rest, n_pages_step):
    del pt_ref
    g_n = n_pages_step
    kp, vp, lp = rest[:g_n], rest[g_n:2 * g_n], rest[2 * g_n:3 * g_n]
    o_ref, m_ref, l_ref, acc_ref, later_ref = rest[3 * g_n:]
    j = pl.program_id(1)
    last = pl.num_programs(1) - 1
    krows = kp[0].shape[0]
    scale = FOX_HEAD_DIM ** -0.5

    @pl.when(j == 0)
    def _():
        m_ref[...] = jnp.full(m_ref.shape, NEG, F32)
        l_ref[...] = jnp.zeros(l_ref.shape, F32)
        acc_ref[...] = jnp.zeros(acc_ref.shape, F32)
        later_ref[...] = jnp.zeros(later_ref.shape, F32)

    q = q_ref[...].astype(BF16)
    cn_col = cncol_ref[...]
    mfar = mfar_ref[...]

    later = later_ref[...]
    scores = [None] * g_n
    for g in reversed(range(g_n)):
        logf = lp[g][...]
        incl, tot = _suffix_by_head(logf, FOX_HEADS)
        decay = (incl - logf) + later
        later = later + tot
        s = lax.dot_general(q, kp[g][...].astype(BF16), NT_DIMS, preferred_element_type=F32)
        scores[g] = s * scale + (cn_col + decay) + mfar
    later_ref[...] = later

    def pv_pages(p):
        out = jnp.dot(p[:, 0:krows], vp[0][...].astype(BF16), preferred_element_type=F32)
        for g in range(1, g_n):
            out = out + jnp.dot(p[:, g * krows:(g + 1) * krows], vp[g][...].astype(BF16),
                                preferred_element_type=F32)
        return out

    _online_update(jnp.concatenate(scores, axis=1), pv_pages, m_ref, l_ref, acc_ref)

    @pl.when(j == last)
    def _():
        s = lax.dot_general(q, kn_ref[...].astype(BF16), NT_DIMS, preferred_element_type=F32)
        s = s * scale + (cn_col - cnrow_ref[...]) + mnew_ref[...]
        vn = vn_ref[...].astype(BF16)
        _online_update(s, lambda p: jnp.dot(p, vn, preferred_element_type=F32),
                       m_ref, l_ref, acc_ref)
        o_ref[...] = acc_ref[...] / l_ref[...]


def _dec_fox(page_table, q, kn, vn, cn_col, cn_row, mfar, mnew, cache_k, cache_v, cache_lf, layer,
             n_pages_step):
    n_seq, rows, _ = q.shape
    n_pages = page_table.shape[1]
    g_n = n_pages_step
    n_steps = n_pages // g_n
    krows = cache_k.shape[2]

    def page_of(b, j, pt, g):
        return pt[b, (n_steps - 1 - j) * g_n + g]

    def page_spec(g):
        return pl.BlockSpec((None, None, krows, HEAD_W),
                            lambda b, j, pt: (layer, page_of(b, j, pt, g), 0, 0))

    def logf_spec(g):
        return pl.BlockSpec((None, None, 1, krows),
                            lambda b, j, pt: (layer, page_of(b, j, pt, g), 0, 0))

    const2 = lambda b, j, pt: (0, 0)
    per_seq = lambda b, j, pt: (b, 0, 0)
    grid_spec = pltpu.PrefetchScalarGridSpec(
        num_scalar_prefetch=1,
        grid=(n_seq, n_steps),
        in_specs=[pl.BlockSpec((None, rows, HEAD_W), per_seq),
                  pl.BlockSpec((None,) + kn.shape[1:], per_seq),
                  pl.BlockSpec((None,) + vn.shape[1:], per_seq),
                  pl.BlockSpec((None,) + cn_col.shape[1:], per_seq),
                  pl.BlockSpec((None,) + cn_row.shape[1:], per_seq),
                  pl.BlockSpec(mfar.shape, const2),
                  pl.BlockSpec(mnew.shape, const2)]
                 + [page_spec(g) for g in range(g_n)] * 2
                 + [logf_spec(g) for g in range(g_n)],
        out_specs=pl.BlockSpec((None, rows, HEAD_W), per_seq),
        scratch_shapes=[pltpu.VMEM((rows, 1), F32), pltpu.VMEM((rows, 1), F32),
                        pltpu.VMEM((rows, HEAD_W), F32), pltpu.VMEM((1, krows), F32)],
    )
    return pl.pallas_call(
        functools.partial(_dec_fox_kernel, n_pages_step=g_n),
        grid_spec=grid_spec,
        out_shape=jax.ShapeDtypeStruct((n_seq, rows, HEAD_W), F32),
        compiler_params=_cparams(2, 56),
        name="dec_fox",
    )(page_table, q, kn, vn, cn_col, cn_row, mfar, mnew,
      *([cache_k] * g_n), *([cache_v] * g_n), *([cache_lf] * g_n))


def _t5_bucket(rel):
    n = jnp.maximum(rel, 0)
    max_exact = NUM_BUCKETS // 2
    large = max_exact + (jnp.log(jnp.maximum(n, 1).astype(F32) / max_exact)
                         / math.log(MAX_DISTANCE / max_exact) * (NUM_BUCKETS - max_exact)).astype(jnp.int32)
    large = jnp.minimum(large, NUM_BUCKETS - 1)
    return jnp.where(n < max_exact, n, large)


def _t5_bias(rel_bias, rel):
    bucket = _t5_bucket(rel)[None]
    n_heads = rel_bias.shape[1]
    out = jnp.zeros((n_heads,) + rel.shape, F32)
    for b in range(NUM_BUCKETS):
        out = jnp.where(bucket == b, rel_bias[b].astype(F32).reshape((n_heads,) + (1,) * rel.ndim), out)
    return out


def _prompt_bias_tiles(rel_bias, tq):
    assert tq >= MAX_DISTANCE
    i = jnp.arange(tq, dtype=jnp.int32)[:, None]
    j = jnp.arange(tq, dtype=jnp.int32)[None, :]
    far = _t5_bias(rel_bias, jnp.full((1, 1), 2 * tq, jnp.int32))
    diag = jnp.where(i >= j, _t5_bias(rel_bias, i - j) - far, NEG)
    sub = _t5_bias(rel_bias, tq + i - j) - far
    return jnp.stack([diag, sub], axis=1)


def _sample_diff_tables(rel_bias, past_len, t_new, new_pad):
    heads = DIFF_HEADS
    cols = PAGE_SIZE * heads
    h_idx = jnp.arange(heads)[:, None, None]
    tok = jnp.arange(t_new, dtype=jnp.int32)
    c_pos = jnp.arange(cols, dtype=jnp.int32) // heads
    c_head = jnp.arange(cols) % heads

    def as_rows(x):
        x = x.reshape(heads * t_new, x.shape[-1])
        return jnp.concatenate([x, x], axis=0)

    match = h_idx == c_head[None, None, :]
    rel_last = (past_len + tok)[:, None] - (past_len - PAGE_SIZE + c_pos)[None, :]
    blast = as_rows(jnp.where(match, _t5_bias(rel_bias, rel_last), NEG))
    far = _t5_bias(rel_bias, jnp.full((1, 1), 2 * MAX_DISTANCE, jnp.int32))
    bfar = as_rows(jnp.where(match, jnp.broadcast_to(far, (heads, t_new, cols)), NEG))
    n_idx = jnp.arange(new_pad, dtype=jnp.int32)
    n_tok = n_idx // heads
    n_head = n_idx % heads
    valid = ((h_idx == n_head[None, None, :]) & (n_tok[None, None, :] <= tok[None, :, None])
             & (n_idx[None, None, :] < t_new * heads))
    rel_new = jnp.maximum(tok[:, None] - n_tok[None, :], 0)
    bnew = as_rows(jnp.where(valid, _t5_bias(rel_bias, rel_new), NEG))
    return bfar, blast, bnew


def _sample_fox_masks(t_new, new_pad):
    heads = FOX_HEADS
    rows = heads * t_new
    r_head = jnp.arange(rows) // t_new
    r_tok = jnp.arange(rows) % t_new
    c_head = jnp.arange(PAGE_SIZE * heads) % heads
    mfar = jnp.where(r_head[:, None] == c_head[None, :], 0.0, NEG).astype(F32)
    n_tok = jnp.arange(new_pad) // heads
    n_head = jnp.arange(new_pad) % heads
    valid = ((r_head[:, None] == n_head[None, :]) & (n_tok[None, :] <= r_tok[:, None])
             & (jnp.arange(new_pad)[None, :] < t_new * heads))
    mnew = jnp.where(valid, 0.0, NEG).astype(F32)
    return mfar, mnew


MIX_TM = 1024
PROJ_TM = 512
FFN_TM = 512
FFN_TF = D_FF
DIFF_TQ = 512
FOX_TQ = 1024
DIFF_PAGES_PER_STEP = 16
FOX_PAGES_PER_STEP = 16
NEW_PAD = 128


def _lam_init(layer):
    return 0.8 - 0.6 * math.exp(-0.3 * layer)


PROJ_TN = 512
EVEN_PLAN = ((F32, 0, 3, False), (F32, 4, 1, True), (F32, 5, 1, True), (BF16, 3, 3, False))
ODD_PLAN = ((F32, 2, 2, True), (F32, 4, 2, True), (BF16, 0, 6, False))


def _prep_weights(w):
    p = dict(w)
    p["w_in_even"] = w["w_in_even"].astype(BF16)
    p["w_out_even"] = w["w_out_even"].astype(BF16)
    w_forget = w["w_in_odd"][..., 3 * FOX_WIDTH:]
    p["w_in_odd"] = w["w_in_odd"][..., :3 * FOX_WIDTH].astype(BF16)
    p["w_forget"] = jnp.pad(w_forget, ((0, 0), (0, 0), (0, HEAD_W - FOX_HEADS))).astype(BF16)
    p["w_out_odd"] = w["w_out_odd"].astype(BF16)
    p["w_gate_ffn"] = w["w_gate_ffn"].astype(BF16)
    p["w_up_ffn"] = w["w_up_ffn"].astype(BF16)
    p["w_down_ffn"] = w["w_down_ffn"].astype(BF16)
    return p


def _run_prompt(x_in, w, bias_tiles):
    n_seq, t, d = x_in.shape
    tm = MIX_TM
    tps = t // tm
    ffn_tps = t // FFN_TM
    x = x_in.reshape(n_seq * t, d)
    outs = {k: [] for k in ("k_diff", "v_diff", "k_fox", "v_fox", "logf", "conv_a", "conv_ffn")}
    for l in range(DEPTH):
        g_mix = w["norm_mix_g"][l][None]
        if l % 2 == 0:
            e = l // 2
            za, zk, zv, zb = _proj(x, g_mix, w["w_in_even"][e], EVEN_PLAN, PROJ_TM, PROJ_TN)
            lam_params = [w[n][e][None] for n in ("lam_q1", "lam_k1", "lam_q2", "lam_k2")]
            ob = _diff_attn(zb, bias_tiles, lam_params, w["subln_g"][e][None], n_seq, t, DIFF_TQ,
                            _lam_init(l))
            prev = jnp.zeros((n_seq, CONV_W - 1, A_WIDTH), F32)
            x, st = _mix_even(x, za, ob, w["conv_a_w"][e], prev, w["w_out_even"][e], tm, tps, 1)
            outs["k_diff"].append(zk.reshape(n_seq, t, DIFF_HEADS, HEAD_W))
            outs["v_diff"].append(zv.reshape(n_seq, t, DIFF_HEADS, HEAD_W))
            outs["conv_a"].append(st[tps - 1::tps])
        else:
            o = l // 2
            zk, zv, zb, f = _proj(x, g_mix, w["w_in_odd"][o], ODD_PLAN, PROJ_TM, PROJ_TN,
                                  w_tail=w["w_forget"][o])
            f_t = f[:, :FOX_HEADS].reshape(n_seq, t, FOX_HEADS)
            f_t = f_t.transpose(0, 2, 1).reshape(n_seq * FOX_HEADS, t)
            b_col = jnp.tile(w["b_forget"][o], n_seq)[:, None]
            logf_t, c_t = _gates_prompt(f_t, b_col)
            c_row = c_t.reshape(n_seq * FOX_HEADS, t // FOX_TQ, 1, FOX_TQ)
            oc = _fox_attn(zb, c_row, n_seq, t, FOX_TQ)
            x = _mix_odd(x, oc, w["w_out_odd"][o], tm)
            outs["k_fox"].append(zk.reshape(n_seq, t, FOX_HEADS, HEAD_W))
            outs["v_fox"].append(zv.reshape(n_seq, t, FOX_HEADS, HEAD_W))
            outs["logf"].append(logf_t.reshape(n_seq, FOX_HEADS, t).transpose(0, 2, 1))
        prev_f = jnp.zeros((n_seq, CONV_W - 1, D_FF), F32)
        x, st = _ffn(x, w["norm_ffn_g"][l][None], w["w_gate_ffn"][l], w["w_up_ffn"][l],
                     w["conv_ffn_w"][l], prev_f, w["w_down_ffn"][l], w["norm_final_g"][None],
                     FFN_TM, FFN_TF, ffn_tps, 1, l == DEPTH - 1)
        outs["conv_ffn"].append(st[ffn_tps - 1::ffn_tps])
    return x.reshape(n_seq, t, d), {k: jnp.stack(v) for k, v in outs.items()}


def _time_major_state(s):
    return s.transpose(1, 0, 2).reshape(1, -1, s.shape[-1])


def _state_from_time_major(s, n_seq):
    return s.reshape(CONV_W - 1, n_seq, s.shape[-1]).transpose(1, 0, 2)


def _run_sample(x_in, past, w):
    (cache_k_diff, cache_v_diff, cache_k_fox, cache_v_fox, cache_logf_fox,
     state_conv_a, state_conv_ffn, page_table) = past
    n_seq, t, d = x_in.shape
    m = n_seq * t
    past_len = page_table.shape[1] * PAGE_SIZE
    ck_diff = cache_k_diff.reshape(cache_k_diff.shape[:2] + (PAGE_SIZE * DIFF_HEADS, HEAD_W))
    cv_diff = cache_v_diff.reshape(cache_v_diff.shape[:2] + (PAGE_SIZE * DIFF_HEADS, HEAD_W))
    ck_fox = cache_k_fox.reshape(cache_k_fox.shape[:2] + (PAGE_SIZE * FOX_HEADS, HEAD_W))
    cv_fox = cache_v_fox.reshape(cache_v_fox.shape[:2] + (PAGE_SIZE * FOX_HEADS, HEAD_W))
    clf_fox = cache_logf_fox.reshape(cache_logf_fox.shape[:2] + (1, PAGE_SIZE * FOX_HEADS))
    bfar, blast, bnew = _sample_diff_tables(w["rel_bias"], past_len, t, NEW_PAD)
    mfar, mnew = _sample_fox_masks(t, NEW_PAD)

    def per_seq(a, heads):
        return a.reshape(t, n_seq, heads, HEAD_W).transpose(1, 0, 2, 3)

    def pad_new(a):
        a = a.reshape(n_seq, -1, HEAD_W)
        return jnp.pad(a, ((0, 0), (0, NEW_PAD - a.shape[1]), (0, 0)))

    x = x_in.transpose(1, 0, 2).reshape(m, d)
    outs = {k: [] for k in ("k_diff", "v_diff", "k_fox", "v_fox", "logf", "conv_a", "conv_ffn")}
    for l in range(DEPTH):
        g_mix = w["norm_mix_g"][l][None]
        if l % 2 == 0:
            e = l // 2
            z, zk, zv, zb = _proj(x, g_mix, w["w_in_even"][e], EVEN_PLAN, m, PROJ_TN)
            q = per_seq(zb[:, :DIFF_QK_WIDTH], DIFF_HEADS)
            k = per_seq(zk, DIFF_HEADS)
            v = per_seq(zv, DIFF_HEADS)
            q_rows = q.transpose(0, 2, 1, 3).reshape(n_seq, DIFF_HEADS * t, HEAD_W)
            q_rows = jnp.concatenate([q_rows, q_rows], axis=1)
            lam_params = [w[n][e][None] for n in ("lam_q1", "lam_k1", "lam_q2", "lam_k2")]
            o_dec = _dec_diff(page_table, q_rows, pad_new(k), pad_new(v), bfar, blast, bnew,
                              lam_params, w["subln_g"][e][None], ck_diff, cv_diff, e,
                              DIFF_PAGES_PER_STEP, _lam_init(l))
            ob = o_dec.reshape(n_seq, DIFF_HEADS, t, HEAD_W).transpose(2, 0, 1, 3)
            ob = ob.reshape(m, B_WIDTH).astype(BF16)
            x, st = _mix_even(x, z, ob, w["conv_a_w"][e], _time_major_state(state_conv_a[e]),
                              w["w_out_even"][e], m, 1, n_seq)
            outs["k_diff"].append(k)
            outs["v_diff"].append(v)
            outs["conv_a"].append(_state_from_time_major(st, n_seq))
        else:
            o = l // 2
            zk, zv, zb, f = _proj(x, g_mix, w["w_in_odd"][o], ODD_PLAN, m, PROJ_TN,
                                  w_tail=w["w_forget"][o])
            q = per_seq(zb[:, :FOX_WIDTH], FOX_HEADS)
            k = per_seq(zk, FOX_HEADS)
            v = per_seq(zv, FOX_HEADS)
            logf, cn = _gates_sample(f[:, :FOX_HEADS], w["b_forget"][o][None], n_seq, t)
            cn = cn.reshape(t, n_seq, FOX_HEADS)
            cn_col = cn.transpose(1, 2, 0).reshape(n_seq, FOX_HEADS * t, 1)
            cn_row = cn.transpose(1, 0, 2).reshape(n_seq, 1, t * FOX_HEADS)
            cn_row = jnp.pad(cn_row, ((0, 0), (0, 0), (0, NEW_PAD - t * FOX_HEADS)))
            q_rows = q.transpose(0, 2, 1, 3).reshape(n_seq, FOX_HEADS * t, HEAD_W)
            o_dec = _dec_fox(page_table, q_rows, pad_new(k), pad_new(v), cn_col, cn_row, mfar, mnew,
                             ck_fox, cv_fox, clf_fox, o, FOX_PAGES_PER_STEP)
            oc = o_dec.reshape(n_seq, FOX_HEADS, t, HEAD_W).transpose(2, 0, 1, 3)
            oc = oc.reshape(m, FOX_WIDTH).astype(BF16)
            x = _mix_odd(x, oc, w["w_out_odd"][o], m)
            outs["k_fox"].append(k)
            outs["v_fox"].append(v)
            outs["logf"].append(logf.reshape(t, n_seq, FOX_HEADS).transpose(1, 0, 2))
        x, st = _ffn(x, w["norm_ffn_g"][l][None], w["w_gate_ffn"][l], w["w_up_ffn"][l],
                     w["conv_ffn_w"][l], _time_major_state(state_conv_ffn[l]), w["w_down_ffn"][l],
                     w["norm_final_g"][None], m, FFN_TF, 1, n_seq, l == DEPTH - 1)
        outs["conv_ffn"].append(_state_from_time_major(st, n_seq))
    y = x.reshape(t, n_seq, d).transpose(1, 0, 2)
    return y, {k: jnp.stack(v) for k, v in outs.items()}


def kernel(x_prompt, x_sample, cache_k_diff, cache_v_diff, cache_k_fox, cache_v_fox, cache_logf_fox,
           state_conv_a, state_conv_ffn, page_table,
           norm_mix_g, norm_ffn_g, norm_final_g, w_in_even, w_out_even, conv_a_w,
           lam_q1, lam_k1, lam_q2, lam_k2, subln_g, rel_bias,
           w_in_odd, b_forget, w_out_odd, w_gate_ffn, w_up_ffn, conv_ffn_w, w_down_ffn):
    w = _prep_weights(dict(
        norm_mix_g=norm_mix_g, norm_ffn_g=norm_ffn_g, norm_final_g=norm_final_g,
        w_in_even=w_in_even, w_out_even=w_out_even, conv_a_w=conv_a_w,
        lam_q1=lam_q1, lam_k1=lam_k1, lam_q2=lam_q2, lam_k2=lam_k2, subln_g=subln_g,
        rel_bias=rel_bias, w_in_odd=w_in_odd, b_forget=b_forget, w_out_odd=w_out_odd,
        w_gate_ffn=w_gate_ffn, w_up_ffn=w_up_ffn, conv_ffn_w=conv_ffn_w, w_down_ffn=w_down_ffn))
    bias_tiles = _prompt_bias_tiles(rel_bias, DIFF_TQ)
    y_p, sp = _run_prompt(x_prompt, w, bias_tiles)
    past = (cache_k_diff, cache_v_diff, cache_k_fox, cache_v_fox, cache_logf_fox,
            state_conv_a, state_conv_ffn, page_table)
    y_s, ss = _run_sample(x_sample, past, w)
    order = ("k_diff", "v_diff", "k_fox", "v_fox", "logf", "conv_a", "conv_ffn")
    return (y_p, y_s) + tuple(sp[k] for k in order) + tuple(ss[k] for k in order)
```

```python
import functools
import math

import jax
import jax.numpy as jnp
from jax import lax
from jax.experimental import pallas as pl
from jax.experimental.pallas import tpu as pltpu

D_MODEL = 1024
DEPTH = 4
PAGE_SIZE = 128
CONV_W = 3
A_WIDTH = D_MODEL // 2
DIFF_HEADS = 4
DIFF_QK_DIM = D_MODEL // 16
DIFF_V_DIM = 2 * DIFF_QK_DIM
HEAD_W = 128
DIFF_QK_WIDTH = DIFF_HEADS * 2 * DIFF_QK_DIM
B_WIDTH = DIFF_HEADS * DIFF_V_DIM
EVEN_IN = 3 * A_WIDTH + 2 * DIFF_QK_WIDTH + B_WIDTH
FOX_HEADS = 8
FOX_HEAD_DIM = D_MODEL // FOX_HEADS
FOX_WIDTH = FOX_HEADS * FOX_HEAD_DIM
NUM_BUCKETS = 32
MAX_DISTANCE = 128
D_FF = (11 * D_MODEL) // 4
EPS = 1e-6
NEG = -1e30
LOG2E = 1.4426950408889634
FOX_HEADS_PER_STEP = 2
DIFF_HEADS_PER_STEP = 4

F32 = jnp.float32
BF16 = jnp.bfloat16
MIB = 1024 * 1024
NT_DIMS = (((1,), (1,)), ((), ()))


def _cparams(n_axes, vmem_mib):
    return pltpu.CompilerParams(dimension_semantics=("arbitrary",) * n_axes,
                                vmem_limit_bytes=vmem_mib * MIB)


def _rms(x, g):
    ms = jnp.mean(x * x, axis=-1, keepdims=True)
    return x * lax.rsqrt(ms + EPS) * g


def _halo_rows(shift):
    return -(-(2 * shift) // 8) * 8


def _proj_kernel(x_ref, g_ref, w_ref, *rest, plan, tn, has_tail):
    outs = rest[has_tail:]
    tm = x_ref.shape[0]
    h = _rms(x_ref[...], g_ref[...]).astype(BF16)
    if has_tail:
        outs[-1][...] = jnp.dot(h, rest[0][...], preferred_element_type=F32)
    heads_per_tile = tn // HEAD_W
    for c in range(w_ref.shape[1] // tn):
        z = jnp.dot(h, w_ref[:, c * tn:(c + 1) * tn], preferred_element_type=F32)
        for o_ref, (dtype, first, count, by_head) in zip(outs, plan):
            if not first <= c < first + count:
                continue
            if by_head:
                n_heads = count * heads_per_tile
                for hh in range(heads_per_tile):
                    hd = (c - first) * heads_per_tile + hh
                    o_ref[pl.ds(hd, tm, stride=n_heads), :] = (
                        z[:, hh * HEAD_W:(hh + 1) * HEAD_W].astype(dtype))
            else:
                o_ref[:, (c - first) * tn:(c - first + 1) * tn] = z.astype(dtype)


def _proj(x, g, w, plan, tm, tn, w_tail=None):
    m, d = x.shape
    has_tail = w_tail is not None
    in_specs = [pl.BlockSpec((tm, d), lambda i: (i, 0)),
                pl.BlockSpec((1, d), lambda i: (0, 0)),
                pl.BlockSpec(w.shape, lambda i: (0, 0))]
    out_specs, out_shape = [], []
    for dtype, _, count, by_head in plan:
        heads = count * tn // HEAD_W
        rows, cols = (heads, HEAD_W) if by_head else (1, count * tn)
        out_specs.append(pl.BlockSpec((tm * rows, cols), lambda i: (i, 0)))
        out_shape.append(jax.ShapeDtypeStruct((m * rows, cols), dtype))
    args = [x, g, w]
    if has_tail:
        in_specs.append(pl.BlockSpec(w_tail.shape, lambda i: (0, 0)))
        out_specs.append(pl.BlockSpec((tm, w_tail.shape[1]), lambda i: (i, 0)))
        out_shape.append(jax.ShapeDtypeStruct((m, w_tail.shape[1]), F32))
        args.append(w_tail)
    return pl.pallas_call(
        functools.partial(_proj_kernel, plan=tuple(plan), tn=tn, has_tail=has_tail),
        grid=(m // tm,),
        in_specs=in_specs,
        out_specs=out_specs,
        out_shape=out_shape,
        compiler_params=_cparams(1, 48),
        name="proj",
    )(*args)


def _softmax_rows(t, m_ref, l_ref, acc_ref, idx):
    m_old = m_ref[idx]
    m_new = jnp.maximum(m_old, jnp.max(t, axis=-1, keepdims=True))
    alpha = jnp.exp2(m_old - m_new)
    p = jnp.exp2(t - jnp.tile(m_new, (1, t.shape[1] // HEAD_W)))
    m_ref[idx] = m_new
    l_ref[idx] = alpha * l_ref[idx] + jnp.sum(p, axis=-1, keepdims=True)
    acc_ref[idx] = alpha * acc_ref[idx]
    return p.astype(BF16)


def _diff_lambda(lq1_ref, lk1_ref, lq2_ref, lk2_ref, lam_init):
    return (jnp.exp(jnp.sum(lq1_ref[...] * lk1_ref[...], axis=-1, keepdims=True))
            - jnp.exp(jnp.sum(lq2_ref[...] * lk2_ref[...], axis=-1, keepdims=True)) + lam_init)


def _diff_attn_kernel(q_ref, k_ref, v_ref, bias_ref, lq1_ref, lk1_ref, lq2_ref, lk2_ref, sg_ref,
                      o_ref, m_ref, l_ref, acc_ref, *, tq, lam_init):
    qi = pl.program_id(2)
    lane = lax.broadcasted_iota(jnp.int32, (tq, HEAD_W), 1)
    heads = [slice(hh * HEAD_W, (hh + 1) * HEAD_W) for hh in range(DIFF_HEADS_PER_STEP)]
    q_maps = []
    for hs in heads:
        qs = q_ref[:, hs].astype(F32) * (DIFF_QK_DIM ** -0.5)
        q_maps.append((jnp.where(lane < DIFF_QK_DIM, qs, 0.0).astype(BF16),
                       jnp.where(lane >= DIFF_QK_DIM, qs, 0.0).astype(BF16)))
    m_ref[...] = jnp.full(m_ref.shape, NEG, F32)
    l_ref[...] = jnp.zeros(l_ref.shape, F32)
    acc_ref[...] = jnp.zeros(acc_ref.shape, F32)

    def step(kt, kind):
        rows = pl.ds(pl.multiple_of(kt * tq, tq), tq)
        for hh, hs in enumerate(heads):
            k = k_ref[rows, hs]
            v = v_ref[rows, hs]
            for mi in range(2):
                t = lax.dot_general(q_maps[hh][mi], k, NT_DIMS, preferred_element_type=F32) * LOG2E
                if kind is not None:
                    t = t + bias_ref[hh, kind] * LOG2E
                p = _softmax_rows(t, m_ref, l_ref, acc_ref, 2 * hh + mi)
                acc_ref[2 * hh + mi] += jnp.dot(p, v, preferred_element_type=F32)

    def far_body(kt, carry):
        step(kt, None)
        return carry

    lax.fori_loop(0, jnp.maximum(qi - 1, 0), far_body, 0)

    @pl.when(qi >= 1)
    def _():
        step(qi - 1, 1)

    step(qi, 0)

    lam = _diff_lambda(lq1_ref, lk1_ref, lq2_ref, lk2_ref, lam_init)
    for hh, hs in enumerate(heads):
        o = acc_ref[2 * hh] / l_ref[2 * hh] - lam * (acc_ref[2 * hh + 1] / l_ref[2 * hh + 1])
        o_ref[:, hs] = (_rms(o, sg_ref[...]) * (1.0 - lam_init)).astype(BF16)


def _diff_attn(zb, bias, lam_params, sg, n_seq, t, tq, lam_init):
    m = zb.shape[0]
    nq = t // tq
    hps = DIFF_HEADS_PER_STEP
    groups = DIFF_HEADS // hps
    width = hps * HEAD_W
    small = pl.BlockSpec((1, DIFF_QK_DIM), lambda b, h, i: (0, 0))
    return pl.pallas_call(
        functools.partial(_diff_attn_kernel, tq=tq, lam_init=lam_init),
        grid=(n_seq, groups, nq),
        in_specs=[pl.BlockSpec((tq, width), lambda b, h, i: (b * nq + i, h)),
                  pl.BlockSpec((t, width), lambda b, h, i: (b, groups + h)),
                  pl.BlockSpec((t, width), lambda b, h, i: (b, 2 * groups + h)),
                  pl.BlockSpec((hps, 2, tq, tq), lambda b, h, i: (h, 0, 0, 0)),
                  small, small, small, small,
                  pl.BlockSpec((1, DIFF_V_DIM), lambda b, h, i: (0, 0))],
        out_specs=pl.BlockSpec((tq, width), lambda b, h, i: (b * nq + i, h)),
        out_shape=jax.ShapeDtypeStruct((m, B_WIDTH), BF16),
        scratch_shapes=[pltpu.VMEM((2 * hps, tq, HEAD_W), F32)] * 3,
        compiler_params=_cparams(3, 52),
        name="diff_attn",
    )(zb, zb, zb, bias, *lam_params, sg)


def _fox_attn_kernel(q_ref, k_ref, v_ref, ck_ref, o_ref, m_ref, l_ref, acc_ref, *, tq):
    qi = pl.program_id(2)
    scale2 = (FOX_HEAD_DIM ** -0.5) * LOG2E
    row = lax.broadcasted_iota(jnp.int32, (tq, tq), 0)
    col = lax.broadcasted_iota(jnp.int32, (tq, tq), 1)
    m_ref[...] = jnp.full(m_ref.shape, NEG, F32)
    l_ref[...] = jnp.zeros(l_ref.shape, F32)
    acc_ref[...] = jnp.zeros(acc_ref.shape, F32)
    heads = [slice(hh * HEAD_W, (hh + 1) * HEAD_W) for hh in range(FOX_HEADS_PER_STEP)]
    qs = [q_ref[:, hs] for hs in heads]
    cqs = [jnp.sum(jnp.where(row == col, ck_ref[hh, qi], 0.0), axis=-1, keepdims=True) * LOG2E
           for hh in range(FOX_HEADS_PER_STEP)]

    def step(kt, diagonal):
        rows = pl.ds(pl.multiple_of(kt * tq, tq), tq)
        for hh, hs in enumerate(heads):
            s = lax.dot_general(qs[hh], k_ref[rows, hs], NT_DIMS, preferred_element_type=F32)
            t = s * scale2 + (cqs[hh] - ck_ref[hh, kt] * LOG2E)
            if diagonal:
                t = jnp.where(row >= col, t, NEG)
            p = _softmax_rows(t, m_ref, l_ref, acc_ref, hh)
            acc_ref[hh] += jnp.dot(p, v_ref[rows, hs], preferred_element_type=F32)

    def far_body(kt, carry):
        step(kt, False)
        return carry

    lax.fori_loop(0, qi, far_body, 0)
    step(qi, True)
    for hh, hs in enumerate(heads):
        o_ref[:, hs] = (acc_ref[hh] / l_ref[hh]).astype(BF16)


def _fox_attn(zb, c_row, n_seq, t, tq):
    m = zb.shape[0]
    nq = t // tq
    hps = FOX_HEADS_PER_STEP
    groups = FOX_HEADS // hps
    width = hps * HEAD_W
    return pl.pallas_call(
        functools.partial(_fox_attn_kernel, tq=tq),
        grid=(n_seq, groups, nq),
        in_specs=[pl.BlockSpec((tq, width), lambda b, h, i: (b * nq + i, h)),
                  pl.BlockSpec((t, width), lambda b, h, i: (b, groups + h)),
                  pl.BlockSpec((t, width), lambda b, h, i: (b, 2 * groups + h)),
                  pl.BlockSpec((hps, nq, 1, tq), lambda b, h, i: (b * groups + h, 0, 0, 0))],
        out_specs=pl.BlockSpec((tq, width), lambda b, h, i: (b * nq + i, h)),
        out_shape=jax.ShapeDtypeStruct((m, FOX_WIDTH), BF16),
        scratch_shapes=[pltpu.VMEM((hps, tq, HEAD_W), F32)] * 3,
        compiler_params=_cparams(3, 40),
        name="fox_attn",
    )(zb, zb, zb, c_row)


def _log_sigmoid(x):
    return -(jnp.maximum(-x, 0.0) + jnp.log1p(jnp.exp(-jnp.abs(x))))


def _split3(x):
    hi = x.astype(BF16)
    r1 = x - hi.astype(F32)
    mid = r1.astype(BF16)
    lo = (r1 - mid.astype(F32)).astype(BF16)
    return hi, mid, lo


def _gates_prompt_kernel(f_ref, b_ref, logf_ref, c_ref, *, chunk):
    rows, t = f_ref.shape
    r = lax.broadcasted_iota(jnp.int32, (chunk, chunk), 0)
    cidx = lax.broadcasted_iota(jnp.int32, (chunk, chunk), 1)
    tri = jnp.where(r <= cidx, 1.0, 0.0).astype(BF16)
    carry = jnp.zeros((rows, 1), F32)
    for ci in range(t // chunk):
        sl = slice(ci * chunk, (ci + 1) * chunk)
        logf = _log_sigmoid(f_ref[:, sl] + b_ref[...])
        logf_ref[:, sl] = logf
        y = carry
        for part in _split3(logf):
            y = y + jnp.dot(part, tri, preferred_element_type=F32)
        c_ref[:, sl] = y
        carry = y[:, chunk - 1:chunk]


def _gates_prompt(f_t, b_col):
    rows, t = f_t.shape
    full = pl.BlockSpec((rows, t), lambda i: (0, 0))
    return pl.pallas_call(
        functools.partial(_gates_prompt_kernel, chunk=256),
        grid=(1,),
        in_specs=[full, pl.BlockSpec((rows, 1), lambda i: (0, 0))],
        out_specs=[full, full],
        out_shape=[jax.ShapeDtypeStruct((rows, t), F32)] * 2,
        compiler_params=_cparams(1, 32),
        name="gates_prompt",
    )(f_t, b_col)


def _gates_sample_kernel(f_ref, b_ref, logf_ref, c_ref, *, n_seq, t):
    logf = _log_sigmoid(f_ref[...] + b_ref[...])
    logf_ref[...] = logf
    run = logf[0:n_seq]
    c_ref[0:n_seq, :] = run
    for ti in range(1, t):
        run = run + logf[ti * n_seq:(ti + 1) * n_seq]
        c_ref[ti * n_seq:(ti + 1) * n_seq, :] = run


def _gates_sample(f, b_row, n_seq, t):
    m, h = f.shape
    full = pl.BlockSpec((m, h), lambda i: (0, 0))
    return pl.pallas_call(
        functools.partial(_gates_sample_kernel, n_seq=n_seq, t=t),
        grid=(1,),
        in_specs=[full, pl.BlockSpec((1, h), lambda i: (0, 0))],
        out_specs=[full, full],
        out_shape=[jax.ShapeDtypeStruct((m, h), F32)] * 2,
        compiler_params=_cparams(1, 32),
        name="gates_sample",
    )(f, b_row)


def _conv3(buf_ref, cw_ref, cur, tm, shift, halo):
    y = buf_ref[pl.ds(halo - 2 * shift, tm), :] * cw_ref[0:1, :]
    y = y + buf_ref[pl.ds(halo - shift, tm), :] * cw_ref[1:2, :]
    return y + cur * cw_ref[2:3, :]


def _mix_even_kernel(x_ref, ab_ref, ac_ref, ah_ref, ob_ref, cw_ref, prev_ref, w_ref,
                     xo_ref, st_ref, ubuf, *, tm, tps, shift):
    halo = _halo_rows(shift)
    first = (pl.program_id(0) % tps) == 0

    @pl.when(first)
    def _():
        ubuf[pl.ds(halo - 2 * shift, 2 * shift), :] = prev_ref[...]

    @pl.when(jnp.logical_not(first))
    def _():
        ubuf[pl.ds(halo - 2 * shift, 2 * shift), :] = ubuf[pl.ds(halo + tm - 2 * shift, 2 * shift), :]

    u = ac_ref[...] * ah_ref[...]
    ubuf[pl.ds(halo, tm), :] = u
    ya = ab_ref[...] * _conv3(ubuf, cw_ref, u, tm, shift, halo)
    st_ref[...] = ubuf[pl.ds(halo + tm - 2 * shift, 2 * shift), :]
    y = jnp.dot(ya.astype(BF16), w_ref[0:A_WIDTH, :], preferred_element_type=F32)
    y = y + jnp.dot(ob_ref[...], w_ref[A_WIDTH:, :], preferred_element_type=F32)
    xo_ref[...] = x_ref[...] + y


def _mix_even(x, z, ob, cw, prev, w, tm, tps, shift):
    m, d = x.shape
    return pl.pallas_call(
        functools.partial(_mix_even_kernel, tm=tm, tps=tps, shift=shift),
        grid=(m // tm,),
        in_specs=[pl.BlockSpec((tm, d), lambda i: (i, 0)),
                  pl.BlockSpec((tm, A_WIDTH), lambda i: (i, 0)),
                  pl.BlockSpec((tm, A_WIDTH), lambda i: (i, 1)),
                  pl.BlockSpec((tm, A_WIDTH), lambda i: (i, 2)),
                  pl.BlockSpec((tm, B_WIDTH), lambda i: (i, 0)),
                  pl.BlockSpec((CONV_W, A_WIDTH), lambda i: (0, 0)),
                  pl.BlockSpec((None, 2 * shift, A_WIDTH), lambda i: (i // tps, 0, 0)),
                  pl.BlockSpec((A_WIDTH + B_WIDTH, d), lambda i: (0, 0))],
        out_specs=[pl.BlockSpec((tm, d), lambda i: (i, 0)),
                   pl.BlockSpec((None, 2 * shift, A_WIDTH), lambda i: (i, 0, 0))],
        out_shape=[jax.ShapeDtypeStruct((m, d), F32),
                   jax.ShapeDtypeStruct((m // tm, 2 * shift, A_WIDTH), F32)],
        scratch_shapes=[pltpu.VMEM((_halo_rows(shift) + tm, A_WIDTH), F32)],
        compiler_params=_cparams(1, 48),
        name="mix_even",
    )(x, z, z, z, ob, cw, prev, w)


def _mix_odd_kernel(x_ref, o_ref, w_ref, xo_ref):
    xo_ref[...] = x_ref[...] + jnp.dot(o_ref[...], w_ref[...], preferred_element_type=F32)


def _mix_odd(x, o, w, tm):
    m, d = x.shape
    return pl.pallas_call(
        _mix_odd_kernel,
        grid=(m // tm,),
        in_specs=[pl.BlockSpec((tm, d), lambda i: (i, 0)),
                  pl.BlockSpec((tm, FOX_WIDTH), lambda i: (i, 0)),
                  pl.BlockSpec((FOX_WIDTH, d), lambda i: (0, 0))],
        out_specs=pl.BlockSpec((tm, d), lambda i: (i, 0)),
        out_shape=jax.ShapeDtypeStruct((m, d), F32),
        compiler_params=_cparams(1, 40),
        name="mix_odd",
    )(x, o, w)


def _ffn_kernel(x_ref, g_ref, wg_ref, wu_ref, cw_ref, prev_ref, wd_ref, gf_ref,
                xo_ref, st_ref, h_ref, acc_ref, gbuf, carry_ref, *, tm, tps, shift, final_norm):
    halo = _halo_rows(shift)
    i = pl.program_id(0)
    j = pl.program_id(1)
    first = (i % tps) == 0

    @pl.when(j == 0)
    def _():
        h_ref[...] = _rms(x_ref[...], g_ref[...]).astype(BF16)
        acc_ref[...] = jnp.zeros(acc_ref.shape, F32)

    @pl.when(first)
    def _():
        gbuf[pl.ds(halo - 2 * shift, 2 * shift), :] = prev_ref[...]

    @pl.when(jnp.logical_not(first))
    def _():
        gbuf[pl.ds(halo - 2 * shift, 2 * shift), :] = carry_ref[j]

    h = h_ref[...]
    gate = jnp.dot(h, wg_ref[...], preferred_element_type=F32)
    up = jnp.dot(h, wu_ref[...], preferred_element_type=F32)
    gbuf[pl.ds(halo, tm), :] = gate
    tail = gbuf[pl.ds(halo + tm - 2 * shift, 2 * shift), :]
    carry_ref[j] = tail
    st_ref[...] = tail
    gc = _conv3(gbuf, cw_ref, gate, tm, shift, halo)
    act = (gc * (0.5 * jnp.tanh(0.5 * gc) + 0.5)) * up
    acc_ref[...] += jnp.dot(act.astype(BF16), wd_ref[...], preferred_element_type=F32)

    @pl.when(j == pl.num_programs(1) - 1)
    def _():
        xn = x_ref[...] + acc_ref[...]
        xo_ref[...] = _rms(xn, gf_ref[...]) if final_norm else xn


def _ffn(x, g, wg, wu, cw, prev, wd, gf, tm, tf, tps, shift, final_norm):
    m, d = x.shape
    f = wg.shape[1]
    return pl.pallas_call(
        functools.partial(_ffn_kernel, tm=tm, tps=tps, shift=shift, final_norm=final_norm),
        grid=(m // tm, f // tf),
        in_specs=[pl.BlockSpec((tm, d), lambda i, j: (i, 0)),
                  pl.BlockSpec((1, d), lambda i, j: (0, 0)),
                  pl.BlockSpec((d, tf), lambda i, j: (0, j)),
                  pl.BlockSpec((d, tf), lambda i, j: (0, j)),
                  pl.BlockSpec((CONV_W, tf), lambda i, j: (0, j)),
                  pl.BlockSpec((None, 2 * shift, tf), lambda i, j: (i // tps, 0, j)),
                  pl.BlockSpec((tf, d), lambda i, j: (j, 0)),
                  pl.BlockSpec((1, d), lambda i, j: (0, 0))],
        out_specs=[pl.BlockSpec((tm, d), lambda i, j: (i, 0)),
                   pl.BlockSpec((None, 2 * shift, tf), lambda i, j: (i, 0, j))],
        out_shape=[jax.ShapeDtypeStruct((m, d), F32),
                   jax.ShapeDtypeStruct((m // tm, 2 * shift, f), F32)],
        scratch_shapes=[pltpu.VMEM((tm, d), BF16), pltpu.VMEM((tm, d), F32),
                        pltpu.VMEM((_halo_rows(shift) + tm, tf), F32),
                        pltpu.VMEM((f // tf, 2 * shift, tf), F32)],
        compiler_params=_cparams(2, 56),
        name="ffn",
    )(x, g, wg, wu, cw, prev, wd, gf)


def _online_update(s, pv_fn, m_ref, l_ref, acc_ref):
    m_old = m_ref[...]
    m_new = jnp.maximum(m_old, jnp.max(s, axis=-1, keepdims=True))
    alpha = jnp.exp(m_old - m_new)
    p = jnp.exp(s - m_new)
    l_ref[...] = alpha * l_ref[...] + jnp.sum(p, axis=-1, keepdims=True)
    acc_ref[...] = alpha * acc_ref[...] + pv_fn(p.astype(BF16))
    m_ref[...] = m_new


def _dec_diff_kernel(pt_ref, q_ref, kn_ref, vn_ref, bfar_ref, blast_ref, bnew_ref,
                     lq1_ref, lk1_ref, lq2_ref, lk2_ref, sg_ref, *rest, n_pages_step, lam_init):
    del pt_ref
    g_n = n_pages_step
    kp, vp = rest[:g_n], rest[g_n:2 * g_n]
    o_ref, m_ref, l_ref, acc_ref = rest[2 * g_n:]
    j = pl.program_id(1)
    last = pl.num_programs(1) - 1
    rows = q_ref.shape[0]
    krows = kp[0].shape[0]

    @pl.when(j == 0)
    def _():
        m_ref[...] = jnp.full(m_ref.shape, NEG, F32)
        l_ref[...] = jnp.zeros(l_ref.shape, F32)
        acc_ref[...] = jnp.zeros(acc_ref.shape, F32)

    lane = lax.broadcasted_iota(jnp.int32, (rows, HEAD_W), 1)
    row = lax.broadcasted_iota(jnp.int32, (rows, HEAD_W), 0)
    keep = (lane // DIFF_QK_DIM) == (row // (rows // 2))
    q = jnp.where(keep, q_ref[...].astype(F32) * (DIFF_QK_DIM ** -0.5), 0.0).astype(BF16)

    scores = []
    for g in range(g_n):
        s = lax.dot_general(q, kp[g][...].astype(BF16), NT_DIMS, preferred_element_type=F32)
        if g == g_n - 1:
            bias = jnp.where(j == last, blast_ref[...], bfar_ref[...])
        else:
            bias = bfar_ref[...]
        scores.append(s + bias)

    def pv_pages(p):
        out = jnp.dot(p[:, 0:krows], vp[0][...].astype(BF16), preferred_element_type=F32)
        for g in range(1, g_n):
            out = out + jnp.dot(p[:, g * krows:(g + 1) * krows], vp[g][...].astype(BF16),
                                preferred_element_type=F32)
        return out

    _online_update(jnp.concatenate(scores, axis=1), pv_pages, m_ref, l_ref, acc_ref)

    @pl.when(j == last)
    def _():
        s = lax.dot_general(q, kn_ref[...].astype(BF16), NT_DIMS, preferred_element_type=F32)
        vn = vn_ref[...].astype(BF16)
        _online_update(s + bnew_ref[...], lambda p: jnp.dot(p, vn, preferred_element_type=F32),
                       m_ref, l_ref, acc_ref)
        a = acc_ref[...] / l_ref[...]
        lam = _diff_lambda(lq1_ref, lk1_ref, lq2_ref, lk2_ref, lam_init)
        o = a[0:rows // 2] - lam * a[rows // 2:rows]
        o_ref[...] = _rms(o, sg_ref[...]) * (1.0 - lam_init)


def _dec_diff(page_table, q, kn, vn, bfar, blast, bnew, lam_params, sg, cache_k, cache_v, layer,
              n_pages_step, lam_init):
    n_seq, rows, _ = q.shape
    n_pages = page_table.shape[1]
    g_n = n_pages_step
    krows = cache_k.shape[2]

    def page_spec(g):
        return pl.BlockSpec((None, None, krows, HEAD_W),
                            lambda b, j, pt: (layer, pt[b, j * g_n + g], 0, 0))

    const2 = lambda b, j, pt: (0, 0)
    per_seq = lambda b, j, pt: (b, 0, 0)
    small = pl.BlockSpec((1, DIFF_QK_DIM), const2)
    grid_spec = pltpu.PrefetchScalarGridSpec(
        num_scalar_prefetch=1,
        grid=(n_seq, n_pages // g_n),
        in_specs=[pl.BlockSpec((None, rows, HEAD_W), per_seq),
                  pl.BlockSpec((None,) + kn.shape[1:], per_seq),
                  pl.BlockSpec((None,) + vn.shape[1:], per_seq),
                  pl.BlockSpec(bfar.shape, const2),
                  pl.BlockSpec(blast.shape, const2),
                  pl.BlockSpec(bnew.shape, const2),
                  small, small, small, small,
                  pl.BlockSpec((1, DIFF_V_DIM), const2)]
                 + [page_spec(g) for g in range(g_n)] * 2,
        out_specs=pl.BlockSpec((None, rows // 2, HEAD_W), per_seq),
        scratch_shapes=[pltpu.VMEM((rows, 1), F32), pltpu.VMEM((rows, 1), F32),
                        pltpu.VMEM((rows, HEAD_W), F32)],
    )
    return pl.pallas_call(
        functools.partial(_dec_diff_kernel, n_pages_step=g_n, lam_init=lam_init),
        grid_spec=grid_spec,
        out_shape=jax.ShapeDtypeStruct((n_seq, rows // 2, HEAD_W), F32),
        compiler_params=_cparams(2, 48),
        name="dec_diff",
    )(page_table, q, kn, vn, bfar, blast, bnew, *lam_params, sg,
      *([cache_k] * g_n), *([cache_v] * g_n))


def _suffix_by_head(x, n_heads):
    width = x.shape[1]
    lane = lax.broadcasted_iota(jnp.int32, x.shape, 1)
    incl = x
    tot = x
    sh = n_heads
    while sh < width:
        shifted = pltpu.roll(incl, width - sh, axis=1)
        incl = incl + jnp.where(lane < width - sh, shifted, 0.0)
        tot = tot + pltpu.roll(tot, sh, axis=1)
        sh *= 2
    return incl, tot


def _dec_fox_kernel(pt_ref, q_ref, kn_ref, vn_ref, cncol_ref, cnrow_ref, mfar_ref, mnew_ref,
                    *rest, n_pages_step):
    del pt_ref
    g_n = n_pages_step
    kp, vp, lp = rest[:g_n], rest[g_n:2 * g_n], rest[2 * g_n:3 * g_n]
    o_ref, m_ref, l_ref, acc_ref, later_ref = rest[3 * g_n:]
    j = pl.program_id(1)
    last = pl.num_programs(1) - 1
    krows = kp[0].shape[0]
    scale = FOX_HEAD_DIM ** -0.5

    @pl.when(j == 0)
    def _():
        m_ref[...] = jnp.full(m_ref.shape, NEG, F32)
        l_ref[...] = jnp.zeros(l_ref.shape, F32)
        acc_ref[...] = jnp.zeros(acc_ref.shape, F32)
        later_ref[...] = jnp.zeros(later_ref.shape, F32)

    q = q_ref[...].astype(BF16)
    cn_col = cncol_ref[...]
    mfar = mfar_ref[...]

    later = later_ref[...]
    scores = [None] * g_n
    for g in reversed(range(g_n)):
        logf = lp[g][...]
        incl, tot = _suffix_by_head(logf, FOX_HEADS)
        decay = (incl - logf) + later
        later = later + tot
        s = lax.dot_general(q, kp[g][...].astype(BF16), NT_DIMS, preferred_element_type=F32)
        scores[g] = s * scale + (cn_col + decay) + mfar
    later_ref[...] = later

    def pv_pages(p):
        out = jnp.dot(p[:, 0:krows], vp[0][...].astype(BF16), preferred_element_type=F32)
        for g in range(1, g_n):
            out = out + jnp.dot(p[:, g * krows:(g + 1) * krows], vp[g][...].astype(BF16),
                                preferred_element_type=F32)
        return out

    _online_update(jnp.concatenate(scores, axis=1), pv_pages, m_ref, l_ref, acc_ref)

    @pl.when(j == last)
    def _():
        s = lax.dot_general(q, kn_ref[...].astype(BF16), NT_DIMS, preferred_element_type=F32)
        s = s * scale + (cn_col - cnrow_ref[...]) + mnew_ref[...]
        vn = vn_ref[...].astype(BF16)
        _online_update(s, lambda p: jnp.dot(p, vn, preferred_element_type=F32),
                       m_ref, l_ref, acc_ref)
        o_ref[...] = acc_ref[...] / l_ref[...]


def _dec_fox(page_table, q, kn, vn, cn_col, cn_row, mfar, mnew, cache_k, cache_v, cache_lf, layer,
             n_pages_step):
    n_seq, rows, _ = q.shape
    n_pages = page_table.shape[1]
    g_n = n_pages_step
    n_steps = n_pages // g_n
    krows = cache_k.shape[2]

    def page_of(b, j, pt, g):
        return pt[b, (n_steps - 1 - j) * g_n + g]

    def page_spec(g):
        return pl.BlockSpec((None, None, krows, HEAD_W),
                            lambda b, j, pt: (layer, page_of(b, j, pt, g), 0, 0))

    def logf_spec(g):
        return pl.BlockSpec((None, None, 1, krows),
                            lambda b, j, pt: (layer, page_of(b, j, pt, g), 0, 0))

    const2 = lambda b, j, pt: (0, 0)
    per_seq = lambda b, j, pt: (b, 0, 0)
    grid_spec = pltpu.PrefetchScalarGridSpec(
        num_scalar_prefetch=1,
        grid=(n_seq, n_steps),
        in_specs=[pl.BlockSpec((None, rows, HEAD_W), per_seq),
                  pl.BlockSpec((None,) + kn.shape[1:], per_seq),
                  pl.BlockSpec((None,) + vn.shape[1:], per_seq),
                  pl.BlockSpec((None,) + cn_col.shape[1:], per_seq),
                  pl.BlockSpec((None,) + cn_row.shape[1:], per_seq),
                  pl.BlockSpec(mfar.shape, const2),
                  pl.BlockSpec(mnew.shape, const2)]
                 + [page_spec(g) for g in range(g_n)] * 2
                 + [logf_spec(g) for g in range(g_n)],
        out_specs=pl.BlockSpec((None, rows, HEAD_W), per_seq),
        scratch_shapes=[pltpu.VMEM((rows, 1), F32), pltpu.VMEM((rows, 1), F32),
                        pltpu.VMEM((rows, HEAD_W), F32), pltpu.VMEM((1, krows), F32)],
    )
    return pl.pallas_call(
        functools.partial(_dec_fox_kernel, n_pages_step=g_n),
        grid_spec=grid_spec,
        out_shape=jax.ShapeDtypeStruct((n_seq, rows, HEAD_W), F32),
        compiler_params=_cparams(2, 56),
        name="dec_fox",
    )(page_table, q, kn, vn, cn_col, cn_row, mfar, mnew,
      *([cache_k] * g_n), *([cache_v] * g_n), *([cache_lf] * g_n))


def _t5_bucket(rel):
    n = jnp.maximum(rel, 0)
    max_exact = NUM_BUCKETS // 2
    large = max_exact + (jnp.log(jnp.maximum(n, 1).astype(F32) / max_exact)
                         / math.log(MAX_DISTANCE / max_exact) * (NUM_BUCKETS - max_exact)).astype(jnp.int32)
    large = jnp.minimum(large, NUM_BUCKETS - 1)
    return jnp.where(n < max_exact, n, large)


def _t5_bias(rel_bias, rel):
    bucket = _t5_bucket(rel)[None]
    n_heads = rel_bias.shape[1]
    out = jnp.zeros((n_heads,) + rel.shape, F32)
    for b in range(NUM_BUCKETS):
        out = jnp.where(bucket == b, rel_bias[b].astype(F32).reshape((n_heads,) + (1,) * rel.ndim), out)
    return out


def _prompt_bias_tiles(rel_bias, tq):
    assert tq >= MAX_DISTANCE
    i = jnp.arange(tq, dtype=jnp.int32)[:, None]
    j = jnp.arange(tq, dtype=jnp.int32)[None, :]
    far = _t5_bias(rel_bias, jnp.full((1, 1), 2 * tq, jnp.int32))
    diag = jnp.where(i >= j, _t5_bias(rel_bias, i - j) - far, NEG)
    sub = _t5_bias(rel_bias, tq + i - j) - far
    return jnp.stack([diag, sub], axis=1)


def _sample_diff_tables(rel_bias, past_len, t_new, new_pad):
    heads = DIFF_HEADS
    cols = PAGE_SIZE * heads
    h_idx = jnp.arange(heads)[:, None, None]
    tok = jnp.arange(t_new, dtype=jnp.int32)
    c_pos = jnp.arange(cols, dtype=jnp.int32) // heads
    c_head = jnp.arange(cols) % heads

    def as_rows(x):
        x = x.reshape(heads * t_new, x.shape[-1])
        return jnp.concatenate([x, x], axis=0)

    match = h_idx == c_head[None, None, :]
    rel_last = (past_len + tok)[:, None] - (past_len - PAGE_SIZE + c_pos)[None, :]
    blast = as_rows(jnp.where(match, _t5_bias(rel_bias, rel_last), NEG))
    far = _t5_bias(rel_bias, jnp.full((1, 1), 2 * MAX_DISTANCE, jnp.int32))
    bfar = as_rows(jnp.where(match, jnp.broadcast_to(far, (heads, t_new, cols)), NEG))
    n_idx = jnp.arange(new_pad, dtype=jnp.int32)
    n_tok = n_idx // heads
    n_head = n_idx % heads
    valid = ((h_idx == n_head[None, None, :]) & (n_tok[None, None, :] <= tok[None, :, None])
             & (n_idx[None, None, :] < t_new * heads))
    rel_new = jnp.maximum(tok[:, None] - n_tok[None, :], 0)
    bnew = as_rows(jnp.where(valid, _t5_bias(rel_bias, rel_new), NEG))
    return bfar, blast, bnew


def _sample_fox_masks(t_new, new_pad):
    heads = FOX_HEADS
    rows = heads * t_new
    r_head = jnp.arange(rows) // t_new
    r_tok = jnp.arange(rows) % t_new
    c_head = jnp.arange(PAGE_SIZE * heads) % heads
    mfar = jnp.where(r_head[:, None] == c_head[None, :], 0.0, NEG).astype(F32)
    n_tok = jnp.arange(new_pad) // heads
    n_head = jnp.arange(new_pad) % heads
    valid = ((r_head[:, None] == n_head[None, :]) & (n_tok[None, :] <= r_tok[:, None])
             & (jnp.arange(new_pad)[None, :] < t_new * heads))
    mnew = jnp.where(valid, 0.0, NEG).astype(F32)
    return mfar, mnew


MIX_TM = 1024
PROJ_TM = 512
FFN_TM = 512
FFN_TF = D_FF
DIFF_TQ = 512
FOX_TQ = 1024
DIFF_PAGES_PER_STEP = 32
FOX_PAGES_PER_STEP = 16
NEW_PAD = 128


def _lam_init(layer):
    return 0.8 - 0.6 * math.exp(-0.3 * layer)


PROJ_TN = 512
EVEN_PLAN = ((F32, 0, 3, False), (F32, 4, 1, True), (F32, 5, 1, True), (BF16, 3, 3, False))
ODD_PLAN = ((F32, 2, 2, True), (F32, 4, 2, True), (BF16, 0, 6, False))


def _prep_weights(w):
    p = dict(w)
    p["w_in_even"] = w["w_in_even"].astype(BF16)
    p["w_out_even"] = w["w_out_even"].astype(BF16)
    w_forget = w["w_in_odd"][..., 3 * FOX_WIDTH:]
    p["w_in_odd"] = w["w_in_odd"][..., :3 * FOX_WIDTH].astype(BF16)
    p["w_forget"] = jnp.pad(w_forget, ((0, 0), (0, 0), (0, HEAD_W - FOX_HEADS))).astype(BF16)
    p["w_out_odd"] = w["w_out_odd"].astype(BF16)
    p["w_gate_ffn"] = w["w_gate_ffn"].astype(BF16)
    p["w_up_ffn"] = w["w_up_ffn"].astype(BF16)
    p["w_down_ffn"] = w["w_down_ffn"].astype(BF16)
    return p


def _run_prompt(x_in, w, bias_tiles):
    n_seq, t, d = x_in.shape
    tm = MIX_TM
    tps = t // tm
    ffn_tps = t // FFN_TM
    x = x_in.reshape(n_seq * t, d)
    outs = {k: [] for k in ("k_diff", "v_diff", "k_fox", "v_fox", "logf", "conv_a", "conv_ffn")}
    for l in range(DEPTH):
        g_mix = w["norm_mix_g"][l][None]
        if l % 2 == 0:
            e = l // 2
            za, zk, zv, zb = _proj(x, g_mix, w["w_in_even"][e], EVEN_PLAN, PROJ_TM, PROJ_TN)
            lam_params = [w[n][e][None] for n in ("lam_q1", "lam_k1", "lam_q2", "lam_k2")]
            ob = _diff_attn(zb, bias_tiles, lam_params, w["subln_g"][e][None], n_seq, t, DIFF_TQ,
                            _lam_init(l))
            prev = jnp.zeros((n_seq, CONV_W - 1, A_WIDTH), F32)
            x, st = _mix_even(x, za, ob, w["conv_a_w"][e], prev, w["w_out_even"][e], tm, tps, 1)
            outs["k_diff"].append(zk.reshape(n_seq, t, DIFF_HEADS, HEAD_W))
            outs["v_diff"].append(zv.reshape(n_seq, t, DIFF_HEADS, HEAD_W))
            outs["conv_a"].append(st[tps - 1::tps])
        else:
            o = l // 2
            zk, zv, zb, f = _proj(x, g_mix, w["w_in_odd"][o], ODD_PLAN, PROJ_TM, PROJ_TN,
                                  w_tail=w["w_forget"][o])
            f_t = f[:, :FOX_HEADS].reshape(n_seq, t, FOX_HEADS)
            f_t = f_t.transpose(0, 2, 1).reshape(n_seq * FOX_HEADS, t)
            b_col = jnp.tile(w["b_forget"][o], n_seq)[:, None]
            logf_t, c_t = _gates_prompt(f_t, b_col)
            c_row = c_t.reshape(n_seq * FOX_HEADS, t // FOX_TQ, 1, FOX_TQ)
            oc = _fox_attn(zb, c_row, n_seq, t, FOX_TQ)
            x = _mix_odd(x, oc, w["w_out_odd"][o], tm)
            outs["k_fox"].append(zk.reshape(n_seq, t, FOX_HEADS, HEAD_W))
            outs["v_fox"].append(zv.reshape(n_seq, t, FOX_HEADS, HEAD_W))
            outs["logf"].append(logf_t.reshape(n_seq, FOX_HEADS, t).transpose(0, 2, 1))
        prev_f = jnp.zeros((n_seq, CONV_W - 1, D_FF), F32)
        x, st = _ffn(x, w["norm_ffn_g"][l][None], w["w_gate_ffn"][l], w["w_up_ffn"][l],
                     w["conv_ffn_w"][l], prev_f, w["w_down_ffn"][l], w["norm_final_g"][None],
                     FFN_TM, FFN_TF, ffn_tps, 1, l == DEPTH - 1)
        outs["conv_ffn"].append(st[ffn_tps - 1::ffn_tps])
    return x.reshape(n_seq, t, d), {k: jnp.stack(v) for k, v in outs.items()}


def _time_major_state(s):
    return s.transpose(1, 0, 2).reshape(1, -1, s.shape[-1])


def _state_from_time_major(s, n_seq):
    return s.reshape(CONV_W - 1, n_seq, s.shape[-1]).transpose(1, 0, 2)


def _run_sample(x_in, past, w):
    (cache_k_diff, cache_v_diff, cache_k_fox, cache_v_fox, cache_logf_fox,
     state_conv_a, state_conv_ffn, page_table) = past
    n_seq, t, d = x_in.shape
    m = n_seq * t
    past_len = page_table.shape[1] * PAGE_SIZE
    ck_diff = cache_k_diff.reshape(cache_k_diff.shape[:2] + (PAGE_SIZE * DIFF_HEADS, HEAD_W))
    cv_diff = cache_v_diff.reshape(cache_v_diff.shape[:2] + (PAGE_SIZE * DIFF_HEADS, HEAD_W))
    ck_fox = cache_k_fox.reshape(cache_k_fox.shape[:2] + (PAGE_SIZE * FOX_HEADS, HEAD_W))
    cv_fox = cache_v_fox.reshape(cache_v_fox.shape[:2] + (PAGE_SIZE * FOX_HEADS, HEAD_W))
    clf_fox = cache_logf_fox.reshape(cache_logf_fox.shape[:2] + (1, PAGE_SIZE * FOX_HEADS))
    bfar, blast, bnew = _sample_diff_tables(w["rel_bias"], past_len, t, NEW_PAD)
    mfar, mnew = _sample_fox_masks(t, NEW_PAD)

    def per_seq(a, heads):
        return a.reshape(t, n_seq, heads, HEAD_W).transpose(1, 0, 2, 3)

    def pad_new(a):
        a = a.reshape(n_seq, -1, HEAD_W)
        return jnp.pad(a, ((0, 0), (0, NEW_PAD - a.shape[1]), (0, 0)))

    x = x_in.transpose(1, 0, 2).reshape(m, d)
    outs = {k: [] for k in ("k_diff", "v_diff", "k_fox", "v_fox", "logf", "conv_a", "conv_ffn")}
    for l in range(DEPTH):
        g_mix = w["norm_mix_g"][l][None]
        if l % 2 == 0:
            e = l // 2
            z, zk, zv, zb = _proj(x, g_mix, w["w_in_even"][e], EVEN_PLAN, m, PROJ_TN)
            q = per_seq(zb[:, :DIFF_QK_WIDTH], DIFF_HEADS)
            k = per_seq(zk, DIFF_HEADS)
            v = per_seq(zv, DIFF_HEADS)
            q_rows = q.transpose(0, 2, 1, 3).reshape(n_seq, DIFF_HEADS * t, HEAD_W)
            q_rows = jnp.concatenate([q_rows, q_rows], axis=1)
            lam_params = [w[n][e][None] for n in ("lam_q1", "lam_k1", "lam_q2", "lam_k2")]
            o_dec = _dec_diff(page_table, q_rows, pad_new(k), pad_new(v), bfar, blast, bnew,
                              lam_params, w["subln_g"][e][None], ck_diff, cv_diff, e,
                              DIFF_PAGES_PER_STEP, _lam_init(l))
            ob = o_dec.reshape(n_seq, DIFF_HEADS, t, HEAD_W).transpose(2, 0, 1, 3)
            ob = ob.reshape(m, B_WIDTH).astype(BF16)
            x, st = _mix_even(x, z, ob, w["conv_a_w"][e], _time_major_state(state_conv_a[e]),
                              w["w_out_even"][e], m, 1, n_seq)
            outs["k_diff"].append(k)
            outs["v_diff"].append(v)
            outs["conv_a"].append(_state_from_time_major(st, n_seq))
        else:
            o = l // 2
            zk, zv, zb, f = _proj(x, g_mix, w["w_in_odd"][o], ODD_PLAN, m, PROJ_TN,
                                  w_tail=w["w_forget"][o])
            q = per_seq(zb[:, :FOX_WIDTH], FOX_HEADS)
            k = per_seq(zk, FOX_HEADS)
            v = per_seq(zv, FOX_HEADS)
            logf, cn = _gates_sample(f[:, :FOX_HEADS], w["b_forget"][o][None], n_seq, t)
            cn = cn.reshape(t, n_seq, FOX_HEADS)
            cn_col = cn.transpose(1, 2, 0).reshape(n_seq, FOX_HEADS * t, 1)
            cn_row = cn.transpose(1, 0, 2).reshape(n_seq, 1, t * FOX_HEADS)
            cn_row = jnp.pad(cn_row, ((0, 0), (0, 0), (0, NEW_PAD - t * FOX_HEADS)))
            q_rows = q.transpose(0, 2, 1, 3).reshape(n_seq, FOX_HEADS * t, HEAD_W)
            o_dec = _dec_fox(page_table, q_rows, pad_new(k), pad_new(v), cn_col, cn_row, mfar, mnew,
                             ck_fox, cv_fox, clf_fox, o, FOX_PAGES_PER_STEP)
            oc = o_dec.reshape(n_seq, FOX_HEADS, t, HEAD_W).transpose(2, 0, 1, 3)
            oc = oc.reshape(m, FOX_WIDTH).astype(BF16)
            x = _mix_odd(x, oc, w["w_out_odd"][o], m)
            outs["k_fox"].append(k)
            outs["v_fox"].append(v)
            outs["logf"].append(logf.reshape(t, n_seq, FOX_HEADS).transpose(1, 0, 2))
        x, st = _ffn(x, w["norm_ffn_g"][l][None], w["w_gate_ffn"][l], w["w_up_ffn"][l],
                     w["conv_ffn_w"][l], _time_major_state(state_conv_ffn[l]), w["w_down_ffn"][l],
                     w["norm_final_g"][None], m, FFN_TF, 1, n_seq, l == DEPTH - 1)
        outs["conv_ffn"].append(_state_from_time_major(st, n_seq))
    y = x.reshape(t, n_seq, d).transpose(1, 0, 2)
    return y, {k: jnp.stack(v) for k, v in outs.items()}


def kernel(x_prompt, x_sample, cache_k_diff, cache_v_diff, cache_k_fox, cache_v_fox, cache_logf_fox,
           state_conv_a, state_conv_ffn, page_table,
           norm_mix_g, norm_ffn_g, norm_final_g, w_in_even, w_out_even, conv_a_w,
           lam_q1, lam_k1, lam_q2, lam_k2, subln_g, rel_bias,
           w_in_odd, b_forget, w_out_odd, w_gate_ffn, w_up_ffn, conv_ffn_w, w_down_ffn):
    w = _prep_weights(dict(
        norm_mix_g=norm_mix_g, norm_ffn_g=norm_ffn_g, norm_final_g=norm_final_g,
        w_in_even=w_in_even, w_out_even=w_out_even, conv_a_w=conv_a_w,
        lam_q1=lam_q1, lam_k1=lam_k1, lam_q2=lam_q2, lam_k2=lam_k2, subln_g=subln_g,
        rel_bias=rel_bias, w_in_odd=w_in_odd, b_forget=b_forget, w_out_odd=w_out_odd,
        w_gate_ffn=w_gate_ffn, w_up_ffn=w_up_ffn, conv_ffn_w=conv_ffn_w, w_down_ffn=w_down_ffn))
    bias_tiles = _prompt_bias_tiles(rel_bias, DIFF_TQ)
    y_p, sp = _run_prompt(x_prompt, w, bias_tiles)
    past = (cache_k_diff, cache_v_diff, cache_k_fox, cache_v_fox, cache_logf_fox,
            state_conv_a, state_conv_ffn, page_table)
    y_s, ss = _run_sample(x_sample, past, w)
    order = ("k_diff", "v_diff", "k_fox", "v_fox", "logf", "conv_a", "conv_ffn")
    return (y_p, y_s) + tuple(sp[k] for k in order) + tuple(ss[k] for k in order)
```
